```python
import math
import jax, jax.numpy as jnp
from jax import lax
import numpy as np

D_MODEL = 2048
BATCH = 4
SEQ = 4096
DEPTH = 2

EPS = 1e-5
RWKV_HEAD = 64
RWKV_HEADS = D_MODEL // 2 // RWKV_HEAD
RWKV_DIM = RWKV_HEADS * RWKV_HEAD
DECAY_LORA = 64
ICLR_LORA = 64
GATE_LORA = 128
RWKV_LN_EPS = 64e-5
RWKV_PROJ = 3 * RWKV_DIM + DECAY_LORA + ICLR_LORA + GATE_LORA
RWKV_SPLITS = [RWKV_DIM, 2 * RWKV_DIM, 3 * RWKV_DIM, 3 * RWKV_DIM + DECAY_LORA,
               3 * RWKV_DIM + DECAY_LORA + ICLR_LORA]
GDN_HEAD = 128
GDN_HEADS = D_MODEL // 2 // GDN_HEAD
GDN_DIM = GDN_HEADS * GDN_HEAD
GDN_CONV = 4
GDN_CHUNK = 64
GDN_PROJ = 4 * GDN_DIM + 2 * GDN_HEADS
MIX_DIM = RWKV_DIM + GDN_DIM
PROJ_EVEN = RWKV_PROJ + GDN_PROJ
SSM_DINNER = 2 * D_MODEL
SSM_HEAD = 64
SSM_HEADS = SSM_DINNER // SSM_HEAD
SSM_GROUPS = 8
SSM_STATE = 128
SSM_CONV = 4
SSM_CHUNK = 128
SSM_CONV_DIM = SSM_DINNER + 2 * SSM_GROUPS * SSM_STATE
SSM_PROJ = SSM_DINNER + SSM_CONV_DIM + SSM_HEADS
FFN_HIDDEN = ((8 * D_MODEL + 767) // 768) * 256

kernel_name = "hybrid_rwkv7_gdn_mamba2_adaln"


def rms_norm(x, eps=EPS):
    xf = x.astype(jnp.float32)
    return (xf * lax.rsqrt(jnp.mean(xf * xf, -1, keepdims=True) + eps)).astype(x.dtype)


def l2_normalize(x, eps=1e-6):
    return x * lax.rsqrt(jnp.sum(x * x, -1, keepdims=True) + eps)


def ada_modulation(c, w, b):
    mod = (jax.nn.silu(c) @ w + b)[:, None, :]
    shift, scale, gate = jnp.split(mod, 3, axis=-1)
    return shift, scale, gate


def token_shift(p):
    return jnp.pad(p, ((0, 0), (1, 0), (0, 0)))[:, :-1]


def causal_conv1d(x, w, b=None):
    K, C = w.shape
    y = lax.conv_general_dilated(x, w.astype(x.dtype)[:, None, :], window_strides=(1,),
                                 padding=[(K - 1, 0)], dimension_numbers=('NWC', 'WIO', 'NWC'),
                                 feature_group_count=C)
    return y if b is None else y + b


def rwkv7_mix(p, mu, w0, w2, a0, a2, g2, k_k, k_a, r_k, ln_w, ln_b):
    p = p.astype(jnp.float32)
    B, T, _ = p.shape
    p = p + mu * (token_shift(p) - p)
    r, k, v, pw, pa, pg = jnp.split(p, RWKV_SPLITS, axis=-1)
    w = -jax.nn.softplus(-(w0 + jnp.tanh(pw) @ w2)) - 0.5
    a = jax.nn.sigmoid(a0 + pa @ a2)
    g = jax.nn.sigmoid(pg) @ g2
    heads = lambda t: t.reshape(B, T, RWKV_HEADS, RWKV_HEAD)
    kk = l2_normalize(heads(k * k_k), 1e-24)
    k = k * (1.0 + (a - 1.0) * k_a)
    r, k, v, a = heads(r), heads(k), heads(v), heads(a)
    decay = jnp.exp(-jnp.exp(heads(w)))
    b = kk * a

    def step(S, inp):
        r_t, d_t, k_t, v_t, kk_t, b_t = inp
        sa = jnp.einsum('bhvk,bhk->bhv', S, kk_t)
        S = S * d_t[:, :, None, :] - sa[..., None] * b_t[:, :, None, :] + v_t[..., None] * k_t[:, :, None, :]
        return S, jnp.einsum('bhvk,bhk->bhv', S, r_t)

    S0 = jnp.zeros((B, RWKV_HEADS, RWKV_HEAD, RWKV_HEAD), jnp.float32)
    xs = tuple(jnp.moveaxis(t, 1, 0) for t in (r, decay, k, v, kk, b))
    _, y = lax.scan(step, S0, xs)
    y = jnp.moveaxis(y, 0, 1)
    mean = jnp.mean(y, -1, keepdims=True)
    var = jnp.mean(jnp.square(y - mean), -1, keepdims=True)
    y = ((y - mean) * lax.rsqrt(var + RWKV_LN_EPS)).reshape(B, T, RWKV_DIM) * ln_w + ln_b
    y = y + (jnp.sum(r * k * r_k, -1, keepdims=True) * v).reshape(B, T, RWKV_DIM)
    return y * g


def chunk_gated_delta_rule(q, k, v, g, beta):
    B, T, H, Dk = q.shape
    Dv = v.shape[-1]
    C = GDN_CHUNK
    N = T // C

    def chunks(t):
        t = jnp.moveaxis(t, 2, 1)
        return t.reshape(B, H, N, C, *t.shape[3:])

    q, k, v, g, beta = map(chunks, (q, k, v, g, beta))
    gc = jnp.cumsum(g, axis=-1)
    causal = jnp.tril(jnp.ones((C, C), bool))
    strict = jnp.tril(jnp.ones((C, C), bool), -1)
    decay = jnp.exp(jnp.where(causal, gc[..., :, None] - gc[..., None, :], -jnp.inf))
    kb = k * beta[..., None]
    a_mat = jnp.where(strict, jnp.einsum('bhnck,bhnsk->bhncs', kb, k) * decay, 0.0)
    eye = jnp.eye(C, dtype=a_mat.dtype)
    rhs = jnp.concatenate([v * beta[..., None], kb * jnp.exp(gc)[..., None]], -1)
    sol = lax.linalg.triangular_solve(a_mat + eye, rhs, left_side=True, lower=True, unit_diagonal=True)
    u, w = sol[..., :Dv], sol[..., Dv:]
    qk = jnp.einsum('bhnck,bhnsk->bhncs', q, k) * decay
    qg = q * jnp.exp(gc)[..., None]
    g_last = gc[..., -1]
    kd = k * jnp.exp(g_last[..., None] - gc)[..., None]

    def step(S, inp):
        u_c, w_c, qk_c, qg_c, kd_c, gl_c = inp
        v_new = u_c - jnp.einsum('bhck,bhkv->bhcv', w_c, S)
        o_c = jnp.einsum('bhck,bhkv->bhcv', qg_c, S) + jnp.einsum('bhcs,bhsv->bhcv', qk_c, v_new)
        S = S * jnp.exp(gl_c)[..., None, None] + jnp.einsum('bhck,bhcv->bhkv', kd_c, v_new)
        return S, o_c

    xs = tuple(jnp.moveaxis(t, 2, 0) for t in (u, w, qk, qg, kd, g_last))
    S0 = jnp.zeros((B, H, Dk, Dv), jnp.float32)
    _, o = lax.scan(step, S0, xs)
    o = jnp.moveaxis(o, 0, 2).reshape(B, H, T, Dv)
    return jnp.moveaxis(o, 1, 2)


def gated_deltanet_mix(p, conv_w, a_log, dt_bias, norm_w):
    p = p.astype(jnp.float32)
    B, T, _ = p.shape
    qkv, z, pb, pa = jnp.split(p, [3 * GDN_DIM, 4 * GDN_DIM, 4 * GDN_DIM + GDN_HEADS], axis=-1)
    qkv = jax.nn.silu(causal_conv1d(qkv, conv_w))
    q, k, v = [t.reshape(B, T, GDN_HEADS, GDN_HEAD) for t in jnp.split(qkv, 3, axis=-1)]
    q = l2_normalize(q) * GDN_HEAD ** -0.5
    k = l2_normalize(k)
    beta = jax.nn.sigmoid(pb)
    g = -jnp.exp(a_log) * jax.nn.softplus(pa + dt_bias)
    o = chunk_gated_delta_rule(q, k, v, g, beta)
    o = rms_norm(o) * norm_w * jax.nn.silu(z.reshape(B, T, GDN_HEADS, GDN_HEAD))
    return o.reshape(B, T, GDN_DIM)


def ssd_chunked_scan(x, dt, A, bm, cm):
    B, T, H, P = x.shape
    G, S = bm.shape[2], bm.shape[3]
    Hg = H // G
    C = SSM_CHUNK
    N = T // C
    xdt = (x * dt[..., None]).reshape(B, N, C, G, Hg, P)
    a = (dt * A).reshape(B, N, C, G, Hg)
    bm = bm.reshape(B, N, C, G, S)
    cm = cm.reshape(B, N, C, G, S)
    causal = jnp.tril(jnp.ones((C, C), bool))[:, :, None, None]

    def step(state, inp):
        x_c, a_c, b_c, c_c = inp
        acum = jnp.cumsum(a_c, axis=1)
        L = jnp.exp(jnp.where(causal, acum[:, :, None] - acum[:, None, :], -jnp.inf))
        cb = jnp.einsum('blgn,bsgn->blsg', c_c, b_c)
        y = jnp.einsum('blsg,blsgh,bsghp->blghp', cb, L, x_c)
        y = y + jnp.einsum('blgn,bghpn->blghp', c_c, state) * jnp.exp(acum)[..., None]
        a_last = acum[:, -1]
        w_s = jnp.exp(a_last[:, None] - acum)
        state = state * jnp.exp(a_last)[..., None, None] + jnp.einsum('bsgn,bsgh,bsghp->bghpn', b_c, w_s, x_c)
        return state, y

    xs = tuple(jnp.moveaxis(t, 1, 0) for t in (xdt, a, bm, cm))
    state0 = jnp.zeros((B, G, Hg, P, S), jnp.float32)
    _, y = lax.scan(step, state0, xs)
    return jnp.moveaxis(y, 0, 1).reshape(B, T, H, P)


def mamba2_mix(h, w_in, conv_w, conv_b, dt_bias, a_log, d_skip, norm_w, w_out):
    B, T, _ = h.shape
    zxbcdt = (h @ w_in).astype(jnp.float32)
    z, xbc, dt = jnp.split(zxbcdt, [SSM_DINNER, SSM_DINNER + SSM_CONV_DIM], axis=-1)
    xbc = jax.nn.silu(causal_conv1d(xbc, conv_w, conv_b))
    xs, bm, cm = jnp.split(xbc, [SSM_DINNER, SSM_DINNER + SSM_GROUPS * SSM_STATE], axis=-1)
    xs = xs.reshape(B, T, SSM_HEADS, SSM_HEAD)
    bm = bm.reshape(B, T, SSM_GROUPS, SSM_STATE)
    cm = cm.reshape(B, T, SSM_GROUPS, SSM_STATE)
    dt = jax.nn.softplus(dt + dt_bias)
    y = ssd_chunked_scan(xs, dt, -jnp.exp(a_log), bm, cm)
    y = y + xs * d_skip[:, None]
    y = (y.reshape(B, T, SSM_DINNER) * jax.nn.silu(z)).reshape(B, T, SSM_GROUPS, SSM_DINNER // SSM_GROUPS)
    y = rms_norm(y).reshape(B, T, SSM_DINNER) * norm_w
    return y.astype(h.dtype) @ w_out


def swiglu(h, w1, w3, w2):
    return (jax.nn.silu(h @ w1) * (h @ w3)) @ w2


def setup_inputs(seed: int = 0) -> dict:
    key = jax.random.key(seed)
    ks = iter(jax.random.split(key, 64))
    nrm = lambda shape, s: jax.random.normal(next(ks), shape, jnp.float32) * s
    uni = lambda shape, lo, hi: jax.random.uniform(next(ks), shape, jnp.float32, lo, hi)
    NE = (DEPTH + 1) // 2
    NO = DEPTH // 2

    def dt_bias(shape):
        dt = jnp.exp(uni(shape, math.log(1e-3), math.log(1e-1)))
        return dt + jnp.log(-jnp.expm1(-dt))

    return {
        "x": nrm((BATCH, SEQ, D_MODEL), 1.0),
        "c": nrm((BATCH, D_MODEL), 1.0),
        "ada_mix_w": nrm((DEPTH, D_MODEL, 3 * D_MODEL), 0.5 * D_MODEL ** -0.5),
        "ada_mix_b": nrm((DEPTH, 3 * D_MODEL), 0.02),
        "ada_ffn_w": nrm((DEPTH, D_MODEL, 3 * D_MODEL), 0.5 * D_MODEL ** -0.5),
        "ada_ffn_b": nrm((DEPTH, 3 * D_MODEL), 0.02),
        "hg_w_in": nrm((NE, D_MODEL, PROJ_EVEN), D_MODEL ** -0.5),
        "hg_w_out": nrm((NE, MIX_DIM, D_MODEL), MIX_DIM ** -0.5),
        "rwkv_mu": uni((NE, RWKV_PROJ), 0.0, 1.0),
        "rwkv_w0": uni((NE, RWKV_DIM), -6.0, -0.5),
        "rwkv_w2": nrm((NE, DECAY_LORA, RWKV_DIM), 0.5 * DECAY_LORA ** -0.5),
        "rwkv_a0": nrm((NE, RWKV_DIM), 0.1),
        "rwkv_a2": nrm((NE, ICLR_LORA, RWKV_DIM), ICLR_LORA ** -0.5),
        "rwkv_g2": nrm((NE, GATE_LORA, RWKV_DIM), GATE_LORA ** -0.5),
        "rwkv_k_k": 0.85 + nrm((NE, RWKV_DIM), 0.02),
        "rwkv_k_a": 1.0 + nrm((NE, RWKV_DIM), 0.02),
        "rwkv_r_k": nrm((NE, RWKV_HEADS, RWKV_HEAD), 0.1),
        "rwkv_ln_w": 1.0 + nrm((NE, RWKV_DIM), 0.02),
        "rwkv_ln_b": nrm((NE, RWKV_DIM), 0.02),
        "gdn_conv_w": nrm((NE, GDN_CONV, 3 * GDN_DIM), GDN_CONV ** -0.5),
        "gdn_a_log": jnp.log(uni((NE, GDN_HEADS), 1.0, 16.0)),
        "gdn_dt_bias": dt_bias((NE, GDN_HEADS)),
        "gdn_norm_w": 1.0 + nrm((NE, GDN_HEAD), 0.02),
        "ssm_w_in": nrm((NO, D_MODEL, SSM_PROJ), D_MODEL ** -0.5),
        "ssm_conv_w": nrm((NO, SSM_CONV, SSM_CONV_DIM), SSM_CONV ** -0.5),
        "ssm_conv_b": nrm((NO, SSM_CONV_DIM), 0.02),
        "ssm_dt_bias": dt_bias((NO, SSM_HEADS)),
        "ssm_a_log": jnp.log(uni((NO, SSM_HEADS), 1.0, 16.0)),
        "ssm_d": 1.0 + nrm((NO, SSM_HEADS), 0.02),
        "ssm_norm_w": 1.0 + nrm((NO, SSM_DINNER), 0.02),
        "ssm_w_out": nrm((NO, SSM_DINNER, D_MODEL), SSM_DINNER ** -0.5),
        "ffn_w1": nrm((DEPTH, D_MODEL, FFN_HIDDEN), D_MODEL ** -0.5),
        "ffn_w3": nrm((DEPTH, D_MODEL, FFN_HIDDEN), D_MODEL ** -0.5),
        "ffn_w2": nrm((DEPTH, FFN_HIDDEN, D_MODEL), FFN_HIDDEN ** -0.5),
        "final_norm_w": 1.0 + nrm((D_MODEL,), 0.02),
    }


def reference(x, c, ada_mix_w, ada_mix_b, ada_ffn_w, ada_ffn_b, hg_w_in, hg_w_out,
              rwkv_mu, rwkv_w0, rwkv_w2, rwkv_a0, rwkv_a2, rwkv_g2, rwkv_k_k, rwkv_k_a,
              rwkv_r_k, rwkv_ln_w, rwkv_ln_b, gdn_conv_w, gdn_a_log, gdn_dt_bias, gdn_norm_w,
              ssm_w_in, ssm_conv_w, ssm_conv_b, ssm_dt_bias, ssm_a_log, ssm_d, ssm_norm_w,
              ssm_w_out, ffn_w1, ffn_w3, ffn_w2, final_norm_w):
    for i in range(DEPTH):
        j = i // 2
        shift, scale, gate = ada_modulation(c, ada_mix_w[i], ada_mix_b[i])
        h = rms_norm(x) * (1.0 + scale) + shift
        if i % 2 == 0:
            p = h @ hg_w_in[j]
            y_a = rwkv7_mix(p[..., :RWKV_PROJ], rwkv_mu[j], rwkv_w0[j], rwkv_w2[j], rwkv_a0[j],
                            rwkv_a2[j], rwkv_g2[j], rwkv_k_k[j], rwkv_k_a[j], rwkv_r_k[j],
                            rwkv_ln_w[j], rwkv_ln_b[j])
            y_b = gated_deltanet_mix(p[..., RWKV_PROJ:], gdn_conv_w[j], gdn_a_log[j],
                                     gdn_dt_bias[j], gdn_norm_w[j])
            y = jnp.concatenate([y_a, y_b], axis=-1).astype(x.dtype) @ hg_w_out[j]
        else:
            y = mamba2_mix(h, ssm_w_in[j], ssm_conv_w[j], ssm_conv_b[j], ssm_dt_bias[j],
                           ssm_a_log[j], ssm_d[j], ssm_norm_w[j], ssm_w_out[j])
        x = x + gate * y
        shift, scale, gate = ada_modulation(c, ada_ffn_w[i], ada_ffn_b[i])
        h = rms_norm(x) * (1.0 + scale) + shift
        x = x + gate * swiglu(h, ffn_w1[i], ffn_w3[i], ffn_w2[i])
    return rms_norm(x) * final_norm_w
```

```python
import functools

import jax
import jax.numpy as jnp
from jax import lax
from jax.experimental import pallas as pl
from jax.experimental.pallas import tpu as pltpu

F32 = jnp.float32
BF16 = jnp.bfloat16

EPS = 1e-5
RWKV_HEAD = 64
RWKV_LN_EPS = 64e-5
GDN_HEAD = 128
GDN_HEADS = 8
SSM_HEAD = 64
SSM_GROUPS = 8
SSM_STATE = 128
CONV_K = 4

LANES = 128
SUBLANES = 8
CHUNK = 64
VMEM_LIMIT = 56 * 1024 * 1024


def _cparams(sem):
    return pltpu.CompilerParams(dimension_semantics=sem, vmem_limit_bytes=VMEM_LIMIT)


def _bf(x):
    return x.astype(BF16)


def _dot(a, b):
    return jnp.dot(a, b, preferred_element_type=F32)


def _dot_nt(a, b):
    return lax.dot_general(a, b, (((1,), (1,)), ((), ())), preferred_element_type=F32)


def _dot_tn(a, b):
    return lax.dot_general(a, b, (((0,), (0,)), ((), ())), preferred_element_type=F32)


def _split2(x):
    hi = _bf(x)
    lo = _bf(x - hi.astype(F32))
    return hi, lo


def _split3(x):
    hi = _bf(x)
    r = x - hi.astype(F32)
    mid = _bf(r)
    lo = _bf(r - mid.astype(F32))
    return hi, mid, lo


def _dot3(a, b):
    ah, al = _split2(a)
    bh, bl = _split2(b)
    return _dot(ah, bh) + (_dot(ah, bl) + _dot(al, bh))


def _dot_sel_l(sel, x):
    hi, mid, lo = _split3(x)
    return _dot(sel, hi) + (_dot(sel, mid) + _dot(sel, lo))


def _dot_sel_r(x, sel):
    hi, mid, lo = _split3(x)
    return _dot(hi, sel) + (_dot(mid, sel) + _dot(lo, sel))


def _silu(x):
    return x * jax.nn.sigmoid(x)


def _softplus(x):
    return jnp.maximum(x, 0.0) + jnp.log(1.0 + jnp.exp(-jnp.abs(x)))


def _iota(shape, dim):
    return lax.broadcasted_iota(jnp.int32, shape, dim)


def _chunk_masks(n, chunk):
    ri = _iota((n, n), 0)
    ci = _iota((n, n), 1)
    same = (ri // chunk) == (ci // chunk)
    return same & (ri >= ci), same & (ri > ci)


def _unit_lower_inverse(a_strict, n_levels):
    n = a_strict.shape[0]
    eye = (_iota((n, n), 0) == _iota((n, n), 1)).astype(F32)
    q = -a_strict
    t = eye + q
    for _ in range(n_levels - 1):
        q = _dot3(q, q)
        t = t + _dot3(t, q)
    return t


def _rows_bcast(x, rows, chunk):
    n = x.shape[0]
    parts = []
    for c0 in range(0, n, chunk):
        parts.append(jnp.broadcast_to(x[c0 + rows:c0 + rows + 1, :], (chunk, x.shape[1])))
    return jnp.concatenate(parts, axis=0) if len(parts) > 1 else parts[0]


def _adaln_kernel(c_ref, w_ref, b_ref, o_ref):
    o_ref[0] = _dot3(_silu(c_ref[...]), w_ref[0]) + b_ref[0]


def _adaln(c_pad, w, b):
    depth, d, n = w.shape
    tn = 1536
    assert n % tn == 0
    return pl.pallas_call(
        _adaln_kernel,
        grid=(depth, n // tn),
        in_specs=[
            pl.BlockSpec((SUBLANES, d), lambda l, j: (0, 0)),
            pl.BlockSpec((1, d, tn), lambda l, j: (l, 0, j)),
            pl.BlockSpec((1, 1, tn), lambda l, j: (l, 0, j)),
        ],
        out_specs=pl.BlockSpec((1, SUBLANES, tn), lambda l, j: (l, 0, j)),
        out_shape=jax.ShapeDtypeStruct((depth, SUBLANES, n), F32),
        compiler_params=_cparams(("parallel", "parallel")),
        name="adaln_mod",
    )(c_pad, w, b.reshape(depth, 1, n))


def _modulated_norm(x, shift, scale):
    ms = jnp.mean(x * x, axis=-1, keepdims=True)
    return (x * lax.rsqrt(ms + EPS)) * (1.0 + scale) + shift


def _norm_proj_kernel(x_ref, shift_ref, scale_ref, w_ref, o_ref, h_ref):
    @pl.when(pl.program_id(2) == 0)
    def _():
        h_ref[...] = _bf(_modulated_norm(x_ref[0], shift_ref[0], scale_ref[0]))

    o_ref[0] = _dot(h_ref[...], w_ref[...])


def _norm_proj(x, shift, scale, w_bf16, tm, tn):
    bsz, t, d = x.shape
    n = w_bf16.shape[1]
    assert t % tm == 0 and n % tn == 0
    return pl.pallas_call(
        _norm_proj_kernel,
        grid=(bsz, t // tm, n // tn),
        in_specs=[
            pl.BlockSpec((1, tm, d), lambda b, i, j: (b, i, 0)),
            pl.BlockSpec((1, 1, d), lambda b, i, j: (b, 0, 0)),
            pl.BlockSpec((1, 1, d), lambda b, i, j: (b, 0, 0)),
            pl.BlockSpec((d, tn), lambda b, i, j: (0, j)),
        ],
        out_specs=pl.BlockSpec((1, tm, tn), lambda b, i, j: (b, i, j)),
        out_shape=jax.ShapeDtypeStruct((bsz, t, n), F32),
        scratch_shapes=[pltpu.VMEM((tm, d), BF16)],
        compiler_params=_cparams(("parallel", "parallel", "arbitrary")),
        name="norm_proj",
    )(x, shift, scale, w_bf16)


def _proj_residual_kernel(n_in, *refs):
    y_refs = refs[:n_in]
    w_refs = refs[n_in:2 * n_in]
    x_ref, gate_ref, o_ref = refs[2 * n_in:]
    acc = _dot(y_refs[0][0], w_refs[0][...])
    for y_ref, w_ref in zip(y_refs[1:], w_refs[1:]):
        acc = acc + _dot(y_ref[0], w_ref[...])
    o_ref[0] = x_ref[0] + gate_ref[0] * acc


def _proj_residual(ys, ws, x, gate, tm, tn):
    bsz, t, d = x.shape
    n_in = len(ys)
    assert t % tm == 0 and d % tn == 0
    in_specs = [pl.BlockSpec((1, tm, y.shape[2]), lambda b, i, j: (b, i, 0)) for y in ys]
    in_specs += [pl.BlockSpec((w.shape[0], tn), lambda b, i, j: (0, j)) for w in ws]
    in_specs += [
        pl.BlockSpec((1, tm, tn), lambda b, i, j: (b, i, j)),
        pl.BlockSpec((1, 1, tn), lambda b, i, j: (b, 0, j)),
    ]
    return pl.pallas_call(
        functools.partial(_proj_residual_kernel, n_in),
        grid=(bsz, t // tm, d // tn),
        in_specs=in_specs,
        out_specs=pl.BlockSpec((1, tm, tn), lambda b, i, j: (b, i, j)),
        out_shape=jax.ShapeDtypeStruct((bsz, t, d), F32),
        compiler_params=_cparams(("parallel", "parallel", "arbitrary")),
        name="proj_residual",
    )(*ys, *ws, x, gate)


def _ffn_kernel(final_norm, x_ref, shift_ref, scale_ref, gate_ref, w1_ref, w3_ref, w2_ref, fw_ref, o_ref,
                h_ref, acc_ref):
    f = pl.program_id(2)

    @pl.when(f == 0)
    def _():
        h_ref[...] = _bf(_modulated_norm(x_ref[0], shift_ref[0], scale_ref[0]))
        acc_ref[...] = jnp.zeros_like(acc_ref)

    h = h_ref[...]
    a = _dot(h, w1_ref[...])
    b = _dot(h, w3_ref[...])
    acc_ref[...] += _dot(_bf(_silu(a) * b), w2_ref[...])

    @pl.when(f == pl.num_programs(2) - 1)
    def _():
        y = x_ref[0] + gate_ref[0] * acc_ref[...]
        if final_norm:
            ms = jnp.mean(y * y, axis=-1, keepdims=True)
            y = (y * lax.rsqrt(ms + EPS)) * fw_ref[...]
        o_ref[0] = y


def _ffn(x, shift, scale, gate, w1, w3, w2, final_w, final_norm, tm, tf):
    bsz, t, d = x.shape
    hidden = w1.shape[1]
    assert t % tm == 0 and hidden % tf == 0
    return pl.pallas_call(
        functools.partial(_ffn_kernel, final_norm),
        grid=(bsz, t // tm, hidden // tf),
        in_specs=[
            pl.BlockSpec((1, tm, d), lambda b, i, f: (b, i, 0)),
            pl.BlockSpec((1, 1, d), lambda b, i, f: (b, 0, 0)),
            pl.BlockSpec((1, 1, d), lambda b, i, f: (b, 0, 0)),
            pl.BlockSpec((1, 1, d), lambda b, i, f: (b, 0, 0)),
            pl.BlockSpec((d, tf), lambda b, i, f: (0, f)),
            pl.BlockSpec((d, tf), lambda b, i, f: (0, f)),
            pl.BlockSpec((tf, d), lambda b, i, f: (f, 0)),
            pl.BlockSpec((1, d), lambda b, i, f: (0, 0)),
        ],
        out_specs=pl.BlockSpec((1, tm, d), lambda b, i, f: (b, i, 0)),
        out_shape=jax.ShapeDtypeStruct((bsz, t, d), F32),
        scratch_shapes=[pltpu.VMEM((tm, d), BF16), pltpu.VMEM((tm, d), F32)],
        compiler_params=_cparams(("parallel", "parallel", "arbitrary")),
        name="swiglu_ffn",
    )(x, shift, scale, gate, w1, w3, w2, final_w)


RWKV_ROWS = 2 * CHUNK


def _head_half_sum(x, first_half):
    s0 = jnp.sum(jnp.where(first_half, x, 0.0), axis=-1, keepdims=True)
    s1 = jnp.sum(jnp.where(first_half, 0.0, x), axis=-1, keepdims=True)
    return jnp.where(first_half, s0, s1)


def _rwkv_kernel(pr_ref, pk_ref, pv_ref, pl_ref, mur_ref, muk_ref, muv_ref, mul_ref, w0_ref, a0_ref, kkw_ref,
                 kaw_ref, rkw_ref, lnw_ref, lnb_ref, w2_ref, a2_ref, g2_ref, o_ref, sh_ref, s_ref):
    ct = RWKV_ROWS
    n_chunks = ct // CHUNK
    t_idx = pl.program_id(2)

    @pl.when(t_idx == 0)
    def _():
        sh_ref[pl.ds(0, SUBLANES), :] = jnp.zeros((SUBLANES, sh_ref.shape[1]), F32)
        s_ref[...] = jnp.zeros_like(s_ref)

    sh_ref[pl.ds(SUBLANES, ct), 0:128] = pr_ref[0]
    sh_ref[pl.ds(SUBLANES, ct), 128:256] = pk_ref[0]
    sh_ref[pl.ds(SUBLANES, ct), 256:384] = pv_ref[0]
    sh_ref[pl.ds(SUBLANES, ct), 384:640] = pl_ref[0]

    def lerp(lo, hi, mu_ref):
        cur = sh_ref[pl.ds(SUBLANES, ct), lo:hi]
        prev = sh_ref[pl.ds(SUBLANES - 1, ct), lo:hi]
        return cur + mu_ref[...] * (prev - cur)

    r = lerp(0, 128, mur_ref)
    k = lerp(128, 256, muk_ref)
    v = lerp(256, 384, muv_ref)
    xl = lerp(384, 640, mul_ref)
    sh_ref[pl.ds(0, SUBLANES), :] = sh_ref[pl.ds(ct, SUBLANES), :]
    pw = xl[:, 0:64]
    pa = xl[:, 64:128]
    pg = xl[:, 128:256]

    w = -_softplus(-(w0_ref[...] + _dot(_bf(jnp.tanh(pw)), w2_ref[...]))) - 0.5
    logd = -jnp.exp(w)
    a = jax.nn.sigmoid(a0_ref[...] + _dot(_bf(pa), a2_ref[...]))
    g = _dot(_bf(jax.nn.sigmoid(pg)), g2_ref[...])

    first_half = _iota((ct, LANES), 1) < RWKV_HEAD
    kk = k * kkw_ref[...]
    kk = kk * lax.rsqrt(_head_half_sum(kk * kk, first_half) + 1e-24)
    k2 = k * (1.0 + (a - 1.0) * kaw_ref[...])
    b = kk * a

    incl_ct, _ = _chunk_masks(ct, CHUNK)
    c = _dot_sel_l(_bf(incl_ct.astype(F32)), logd)
    cm1 = c - logd
    cref = _rows_bcast(c, CHUNK // 2 - 1, CHUNK)
    cend = _rows_bcast(c, CHUNK - 1, CHUNK)
    e_neg = jnp.exp(cref - c)
    rt = r * jnp.exp(c - cref)
    kkt = kk * jnp.exp(cm1 - cref)
    bt = b * e_neg
    kt = k2 * e_neg
    kk0 = kk * jnp.exp(cm1)
    r0 = r * jnp.exp(c)
    e_end = jnp.exp(cend - c)
    bh = b * e_end
    kh = k2 * e_end
    dc = jnp.exp(cend)

    fh = _iota((CHUNK, LANES), 1) < RWKV_HEAD

    def stack(x, masked):
        parts = []
        for ci in range(n_chunks):
            xc = x[ci * CHUNK:(ci + 1) * CHUNK]
            if masked:
                parts += [jnp.where(fh, xc, 0.0), jnp.where(fh, 0.0, xc)]
            else:
                parts += [xc, xc]
        return jnp.concatenate(parts, axis=0)

    def unstack(x_st, ci):
        base = 2 * ci * CHUNK
        return x_st[base:base + CHUNK] + x_st[base + CHUNK:base + 2 * CHUNK]

    ns = 2 * ct
    lhs = _bf(jnp.concatenate([stack(kkt, True), stack(rt, True)], axis=0))
    rhs = _bf(jnp.concatenate([stack(bt, False), stack(kt, False)], axis=0))
    abig = _dot_nt(lhs, rhs)
    incl, strict = _chunk_masks(ns, CHUNK)
    a_ab = jnp.where(strict, abig[:ns, :ns], 0.0)
    a_ak = jnp.where(strict, abig[:ns, ns:], 0.0)
    a_rb = jnp.where(incl, abig[ns:, :ns], 0.0)
    a_rk = jnp.where(incl, abig[ns:, ns:], 0.0)

    tmat = _unit_lower_inverse(a_ab, 6)
    v_st = _bf(stack(v, True))
    av = _dot(_bf(a_ak), v_st)
    sol = _dot3(tmat, jnp.concatenate([stack(kk0, True), av], axis=1))
    ar = _dot(_bf(a_rb), _bf(sol))
    rm_st = stack(r0, True) - ar[:, :LANES]
    yv_st = _dot(_bf(a_rk), v_st) - ar[:, LANES:]

    ri = _iota((LANES, LANES), 0)
    cj = _iota((LANES, LANES), 1)
    same_head = (ri // RWKV_HEAD) == (cj // RWKV_HEAD)

    ys = []
    for ci in range(n_chunks):
        rows = slice(ci * CHUNK, (ci + 1) * CHUNK)
        wm_c = unstack(sol[:, :LANES], ci)
        uv_c = unstack(sol[:, LANES:], ci)
        s_old = s_ref[...]
        s_bf = _bf(s_old)
        ys.append(_dot_nt(_bf(unstack(rm_st, ci)), s_bf) + unstack(yv_st, ci))
        m_t = jnp.where(same_head, _dot_tn(_bf(wm_c), _bf(bh[rows])), 0.0)
        hv_t = jnp.where(
            same_head,
            _dot_tn(_bf(jnp.concatenate([v[rows], -uv_c], axis=0)),
                    _bf(jnp.concatenate([kh[rows], bh[rows]], axis=0))),
            0.0)
        s_ref[...] = s_old * dc[ci * CHUNK:ci * CHUNK + 1, :] - _dot(s_bf, _bf(m_t)) + hv_t
    y = jnp.concatenate(ys, axis=0)

    inv_n = 1.0 / RWKV_HEAD
    mean = _head_half_sum(y, first_half) * inv_n
    yc = y - mean
    var = _head_half_sum(yc * yc, first_half) * inv_n
    yn = yc * lax.rsqrt(var + RWKV_LN_EPS) * lnw_ref[...] + lnb_ref[...]
    bonus = _head_half_sum(r * k2 * rkw_ref[...], first_half) * v
    o_ref[0] = _bf((yn + bonus) * g)


def _rwkv_mix(p, col0, params):
    bsz, t, _ = p.shape
    dim = params["w0"].shape[1]
    nb = dim // LANES
    ct = RWKV_ROWS
    assert t % ct == 0 and col0 % 256 == 0
    cb0 = col0 // LANES
    lb0 = (col0 + 3 * dim) // 256

    def col_spec(off):
        return pl.BlockSpec((1, ct, LANES), lambda b, h, i: (b, i, cb0 + off + h))

    def row_spec(off):
        return pl.BlockSpec((1, LANES), lambda b, h, i: (0, off + h))

    mu = params["mu"]
    in_specs = [
        col_spec(0), col_spec(nb), col_spec(2 * nb),
        pl.BlockSpec((1, ct, 256), lambda b, h, i: (b, i, lb0)),
        row_spec(0), row_spec(nb), row_spec(2 * nb),
        pl.BlockSpec((1, 256), lambda b, h, i: (0, 3 * dim // 256)),
        row_spec(0), row_spec(0), row_spec(0), row_spec(0), row_spec(0), row_spec(0), row_spec(0),
        pl.BlockSpec((params["w2"].shape[0], LANES), lambda b, h, i: (0, h)),
        pl.BlockSpec((params["a2"].shape[0], LANES), lambda b, h, i: (0, h)),
        pl.BlockSpec((params["g2"].shape[0], LANES), lambda b, h, i: (0, h)),
    ]
    return pl.pallas_call(
        _rwkv_kernel,
        grid=(bsz, nb, t // ct),
        in_specs=in_specs,
        out_specs=pl.BlockSpec((1, ct, LANES), lambda b, h, i: (b, i, h)),
        out_shape=jax.ShapeDtypeStruct((bsz, t, dim), BF16),
        scratch_shapes=[pltpu.VMEM((ct + SUBLANES, 640), F32), pltpu.VMEM((LANES, LANES), F32)],
        compiler_params=_cparams(("parallel", "parallel", "arbitrary")),
        name="rwkv7_mix",
    )(p, p, p, p, mu, mu, mu, mu, params["w0"], params["a0"], params["k_k"], params["k_a"], params["r_k"],
      params["ln_w"], params["ln_b"], params["w2"], params["a2"], params["g2"])


GDN_ROWS = 4 * CHUNK


def _causal_conv(sh_ref, lo, hi, w, n_rows):
    acc = None
    for j in range(CONV_K):
        term = sh_ref[pl.ds(SUBLANES - (CONV_K - 1) + j, n_rows), lo:hi] * w[j:j + 1, :]
        acc = term if acc is None else acc + term
    return acc


def _gdn_kernel(q_ref, k_ref, v_ref, z_ref, gt_ref, cwq_ref, cwk_ref, cwv_ref, alog_ref, dtb_ref, nw_ref,
                o_ref, sh_ref, s_ref):
    ct = GDN_ROWS
    n_chunks = ct // CHUNK
    h_idx = pl.program_id(1)
    t_idx = pl.program_id(2)

    @pl.when(t_idx == 0)
    def _():
        sh_ref[pl.ds(0, SUBLANES), :] = jnp.zeros((SUBLANES, sh_ref.shape[1]), F32)
        s_ref[...] = jnp.zeros_like(s_ref)

    sh_ref[pl.ds(SUBLANES, ct), 0:128] = q_ref[0]
    sh_ref[pl.ds(SUBLANES, ct), 128:256] = k_ref[0]
    sh_ref[pl.ds(SUBLANES, ct), 256:384] = v_ref[0]
    q = _silu(_causal_conv(sh_ref, 0, 128, cwq_ref[...], ct))
    k = _silu(_causal_conv(sh_ref, 128, 256, cwk_ref[...], ct))
    v = _silu(_causal_conv(sh_ref, 256, 384, cwv_ref[...], ct))
    sh_ref[pl.ds(0, SUBLANES), :] = sh_ref[pl.ds(ct, SUBLANES), :]

    q = q * lax.rsqrt(jnp.sum(q * q, axis=-1, keepdims=True) + 1e-6) * (GDN_HEAD ** -0.5)
    k = k * lax.rsqrt(jnp.sum(k * k, axis=-1, keepdims=True) + 1e-6)

    gates = gt_ref[0]
    lane = _iota((ct, LANES), 1)
    beta_all = jax.nn.sigmoid(gates)
    g_all = -jnp.exp(alog_ref[...]) * _softplus(gates + dtb_ref[...])
    beta = jnp.sum(jnp.where(lane == h_idx, beta_all, 0.0), axis=-1, keepdims=True)
    g = jnp.sum(jnp.where(lane == h_idx + GDN_HEADS, g_all, 0.0), axis=-1, keepdims=True)

    incl, strict = _chunk_masks(ct, CHUNK)
    gc = _dot_sel_l(_bf(incl.astype(F32)), jnp.broadcast_to(g, (ct, LANES)))
    gcw = jnp.concatenate([gc] * (ct // LANES), axis=1)
    decay = jnp.where(incl, jnp.exp(jnp.minimum(gcw - gcw.T, 0.0)), 0.0)

    kb = k * beta
    kq = _dot_nt(_bf(jnp.concatenate([kb, q], axis=0)), _bf(k))
    a_mat = jnp.where(strict, kq[:ct] * decay, 0.0)
    qk = kq[ct:] * decay
    tmat = _unit_lower_inverse(a_mat, 6)

    egc = jnp.exp(gc)
    sol = _dot3(tmat, jnp.concatenate([v * beta, kb * egc], axis=1))
    qu = _dot(_bf(qk), _bf(sol))
    o_v = qu[:, :LANES]
    o_m = q * egc - qu[:, LANES:]
    gl = _rows_bcast(gc, CHUNK - 1, CHUNK)
    kd = k * jnp.exp(gl - gc)
    egl = jnp.exp(gl)

    outs = []
    for ci in range(n_chunks):
        rows = slice(ci * CHUNK, (ci + 1) * CHUNK)
        s_old = s_ref[...]
        s_bf = _bf(s_old)
        outs.append(_dot(_bf(o_m[rows]), s_bf) + o_v[rows])
        hm = _dot_tn(_bf(kd[rows]), _bf(sol[rows]))
        s_ref[...] = s_old * egl[ci * CHUNK:ci * CHUNK + 1, :] - _dot(_bf(hm[:, LANES:]), s_bf) + hm[:, :LANES]
    o = jnp.concatenate(outs, axis=0)

    ms = jnp.mean(o * o, axis=-1, keepdims=True)
    o_ref[0] = _bf((o * lax.rsqrt(ms + EPS)) * nw_ref[...] * _silu(z_ref[0]))


def _gdn_mix(p, col0, params):
    bsz, t, _ = p.shape
    ct = GDN_ROWS
    nb = GDN_HEADS
    assert t % ct == 0 and col0 % LANES == 0
    cb0 = col0 // LANES

    def col_spec(off):
        return pl.BlockSpec((1, ct, LANES), lambda b, h, i: (b, i, cb0 + off + h))

    def cw_spec(off):
        return pl.BlockSpec((CONV_K, LANES), lambda b, h, i: (0, off + h))

    full_row = pl.BlockSpec((1, LANES), lambda b, h, i: (0, 0))
    cw = params["conv_w"]
    return pl.pallas_call(
        _gdn_kernel,
        grid=(bsz, nb, t // ct),
        in_specs=[
            col_spec(0), col_spec(nb), col_spec(2 * nb), col_spec(3 * nb),
            pl.BlockSpec((1, ct, LANES), lambda b, h, i: (b, i, cb0 + 4 * nb)),
            cw_spec(0), cw_spec(nb), cw_spec(2 * nb),
            full_row, full_row, full_row,
        ],
        out_specs=pl.BlockSpec((1, ct, LANES), lambda b, h, i: (b, i, h)),
        out_shape=jax.ShapeDtypeStruct((bsz, t, nb * GDN_HEAD), BF16),
        scratch_shapes=[pltpu.VMEM((ct + SUBLANES, 384), F32), pltpu.VMEM((GDN_HEAD, GDN_HEAD), F32)],
        compiler_params=_cparams(("parallel", "parallel", "arbitrary")),
        name="gated_deltanet_mix",
    )(p, p, p, p, p, cw, cw, cw, params["a_log_row"], params["dt_bias_row"], params["norm_w"])


SSD_CHUNK = 128
SSD_GROUP_CH = 512


def _ssd_kernel(z_ref, x_ref, b_ref, c_ref, dt_ref, cwx_ref, cwb_ref, cwc_ref, cbx_ref, cbb_ref, cbc_ref,
                dtb_ref, alog_ref, alogs_ref, dsk_ref, nw_ref, o_ref, sh_ref, at_ref, st_ref):
    ct = SSD_CHUNK
    gch = SSD_GROUP_CH
    hpg = gch // SSM_HEAD
    g_idx = pl.program_id(1)
    t_idx = pl.program_id(2)

    @pl.when(t_idx == 0)
    def _():
        sh_ref[pl.ds(0, SUBLANES), :] = jnp.zeros((SUBLANES, sh_ref.shape[1]), F32)
        st_ref[...] = jnp.zeros_like(st_ref)

    sh_ref[pl.ds(SUBLANES, ct), 0:gch] = x_ref[0]
    sh_ref[pl.ds(SUBLANES, ct), gch:gch + LANES] = b_ref[0]
    sh_ref[pl.ds(SUBLANES, ct), gch + LANES:gch + 2 * LANES] = c_ref[0]
    xs = _silu(_causal_conv(sh_ref, 0, gch, cwx_ref[...], ct) + cbx_ref[...])
    bm = _silu(_causal_conv(sh_ref, gch, gch + LANES, cwb_ref[...], ct) + cbb_ref[...])
    cm = _silu(_causal_conv(sh_ref, gch + LANES, gch + 2 * LANES, cwc_ref[...], ct) + cbc_ref[...])
    sh_ref[pl.ds(0, SUBLANES), :] = sh_ref[pl.ds(ct, SUBLANES), :]

    dtv = _softplus(dt_ref[0] + dtb_ref[...])
    expand = _bf((_iota((LANES, gch), 0) == g_idx * hpg + _iota((LANES, gch), 1) // SSM_HEAD).astype(F32))
    dt_exp = _dot_sel_r(dtv, expand)
    a_exp = dt_exp * (-jnp.exp(alog_ref[...]))
    xdt = xs * dt_exp

    ri = _iota((ct, ct), 0)
    ci = _iota((ct, ct), 1)
    incl = ri >= ci
    acum = _dot_sel_l(_bf(incl.astype(F32)), a_exp)
    at_ref[...] = (dtv * (-jnp.exp(alogs_ref[...]))).T
    a_rows = at_ref[pl.ds(pl.multiple_of(g_idx * hpg, hpg), hpg), :]
    acum_rows = _dot_sel_r(a_rows, _bf((ri <= ci).astype(F32)))

    cb = _dot_nt(_bf(cm), _bf(bm))
    st_old = st_ref[...]
    y_state = _dot(_bf(cm), _bf(st_old)) * jnp.exp(acum)

    first_half = _iota((ct, LANES), 1) < SSM_HEAD
    y_parts = []
    for jp in range(hpg // 2):
        x_pair = _bf(xdt[:, jp * LANES:(jp + 1) * LANES])
        pair = []
        for j in (2 * jp, 2 * jp + 1):
            col = jnp.broadcast_to(acum[:, j * SSM_HEAD:j * SSM_HEAD + 1], (ct, ct))
            row = jnp.broadcast_to(acum_rows[j:j + 1, :], (ct, ct))
            lmat = jnp.where(incl, jnp.exp(jnp.minimum(col - row, 0.0)), 0.0)
            pair.append(_dot(_bf(cb * lmat), x_pair))
        y_parts.append(jnp.where(first_half, pair[0], pair[1]))
    y = jnp.concatenate(y_parts, axis=1) + y_state + xs * dsk_ref[...]

    a_last = acum[ct - 1:ct, :]
    st_ref[...] = st_old * jnp.exp(a_last) + _dot_tn(_bf(bm), _bf(xdt * jnp.exp(a_last - acum)))

    yg = y * _silu(z_ref[0])
    ms = jnp.mean(yg * yg, axis=-1, keepdims=True)
    o_ref[0] = _bf((yg * lax.rsqrt(ms + EPS)) * nw_ref[...])


def _ssd_mix(zx, params):
    bsz, t, _ = zx.shape
    ct = SSD_CHUNK
    gch = SSD_GROUP_CH
    d_inner = params["norm_w"].shape[1]
    ng = d_inner // gch
    assert t % ct == 0 and ng == SSM_GROUPS
    xb = d_inner // gch
    bb = 2 * d_inner // LANES
    cbk = bb + ng
    dtb = cbk + ng

    cw = params["conv_w"]
    cbias = params["conv_b"]
    wide = lambda off: pl.BlockSpec((1, ct, gch), lambda b, g, i: (b, i, off + g))
    narrow = lambda off: pl.BlockSpec((1, ct, LANES), lambda b, g, i: (b, i, off + g))
    row_wide = pl.BlockSpec((1, gch), lambda b, g, i: (0, g))
    row_full = pl.BlockSpec((1, LANES), lambda b, g, i: (0, 0))
    cb_b = d_inner // LANES
    return pl.pallas_call(
        _ssd_kernel,
        grid=(bsz, ng, t // ct),
        in_specs=[
            wide(0), wide(xb), narrow(bb), narrow(cbk),
            pl.BlockSpec((1, ct, LANES), lambda b, g, i: (b, i, dtb)),
            pl.BlockSpec((CONV_K, gch), lambda b, g, i: (0, g)),
            pl.BlockSpec((CONV_K, LANES), lambda b, g, i: (0, cb_b + g)),
            pl.BlockSpec((CONV_K, LANES), lambda b, g, i: (0, cb_b + ng + g)),
            pl.BlockSpec((1, gch), lambda b, g, i: (0, g)),
            pl.BlockSpec((1, LANES), lambda b, g, i: (0, cb_b + g)),
            pl.BlockSpec((1, LANES), lambda b, g, i: (0, cb_b + ng + g)),
            row_full, row_wide, row_full, row_wide, row_wide,
        ],
        out_specs=pl.BlockSpec((1, ct, gch), lambda b, g, i: (b, i, g)),
        out_shape=jax.ShapeDtypeStruct((bsz, t, d_inner), BF16),
        scratch_shapes=[
            pltpu.VMEM((ct + SUBLANES, gch + 2 * LANES), F32),
            pltpu.VMEM((LANES, ct), F32),
            pltpu.VMEM((SSM_STATE, gch), F32),
        ],
        compiler_params=_cparams(("parallel", "parallel", "arbitrary")),
        name="mamba2_ssd_mix",
    )(zx, zx, zx, zx, zx, cw, cw, cw, cbias, cbias, cbias, params["dt_bias_row"], params["a_log_exp"],
      params["a_log_row"], params["d_exp"], params["norm_w"])


def _pad_cols(w, n):
    return jnp.pad(w, ((0, 0), (0, n - w.shape[1])))


def _lane_row(v, offset):
    return jnp.pad(v, (offset, LANES - offset - v.shape[0])).reshape(1, LANES)


def _mods(mod, bsz, d):
    mod = mod[:bsz]
    return mod[:, None, 0:d], mod[:, None, d:2 * d], mod[:, None, 2 * d:3 * d]


def kernel(x, c, ada_mix_w, ada_mix_b, ada_ffn_w, ada_ffn_b, hg_w_in, hg_w_out, rwkv_mu, rwkv_w0, rwkv_w2,
           rwkv_a0, rwkv_a2, rwkv_g2, rwkv_k_k, rwkv_k_a, rwkv_r_k, rwkv_ln_w, rwkv_ln_b, gdn_conv_w,
           gdn_a_log, gdn_dt_bias, gdn_norm_w, ssm_w_in, ssm_conv_w, ssm_conv_b, ssm_dt_bias, ssm_a_log,
           ssm_d, ssm_norm_w, ssm_w_out, ffn_w1, ffn_w3, ffn_w2, final_norm_w):
    bsz, t, d = x.shape
    depth = ada_mix_w.shape[0]
    assert bsz <= SUBLANES

    c_pad = jnp.pad(c, ((0, SUBLANES - bsz), (0, 0)))
    mix_mod = _adaln(c_pad, ada_mix_w, ada_mix_b)
    ffn_mod = _adaln(c_pad, ada_ffn_w, ada_ffn_b)
    final_w = final_norm_w.reshape(1, d)

    for i in range(depth):
        j = i // 2
        shift, scale, gate = _mods(mix_mod[i], bsz, d)
        if i % 2 == 0:
            rwkv_dim = rwkv_w0.shape[1]
            gdn_dim = GDN_HEADS * GDN_HEAD
            rwkv_cols = 3 * rwkv_dim + rwkv_w2.shape[1] + rwkv_a2.shape[1] + rwkv_g2.shape[1]
            n_even = 7680
            p = _norm_proj(x, shift, scale, _bf(_pad_cols(hg_w_in[j], n_even)), tm=1024, tn=1280)
            row = lambda v: v.reshape(1, -1)
            y_a = _rwkv_mix(p, 0, dict(
                mu=row(rwkv_mu[j]), w0=row(rwkv_w0[j]), a0=row(rwkv_a0[j]), k_k=row(rwkv_k_k[j]),
                k_a=row(rwkv_k_a[j]), r_k=row(rwkv_r_k[j]), ln_w=row(rwkv_ln_w[j]), ln_b=row(rwkv_ln_b[j]),
                w2=_bf(rwkv_w2[j]), a2=_bf(rwkv_a2[j]), g2=_bf(rwkv_g2[j])))
            y_b = _gdn_mix(p, rwkv_cols, dict(
                conv_w=gdn_conv_w[j], a_log_row=_lane_row(gdn_a_log[j], GDN_HEADS),
                dt_bias_row=_lane_row(gdn_dt_bias[j], GDN_HEADS), norm_w=row(gdn_norm_w[j])))
            w_out = _bf(hg_w_out[j])
            x = _proj_residual([y_a, y_b], [w_out[:rwkv_dim], w_out[rwkv_dim:rwkv_dim + gdn_dim]], x, gate,
                               tm=1024, tn=512)
        else:
            d_inner = ssm_norm_w.shape[1]
            zx = _norm_proj(x, shift, scale, _bf(_pad_cols(ssm_w_in[j], 10752)), tm=1024, tn=1536)
            y = _ssd_mix(zx, dict(
                conv_w=ssm_conv_w[j], conv_b=ssm_conv_b[j].reshape(1, -1),
                dt_bias_row=_lane_row(ssm_dt_bias[j], 0), a_log_exp=jnp.repeat(ssm_a_log[j], SSM_HEAD).reshape(1, -1),
                a_log_row=_lane_row(ssm_a_log[j], 0), d_exp=jnp.repeat(ssm_d[j], SSM_HEAD).reshape(1, -1),
                norm_w=ssm_norm_w[j].reshape(1, d_inner)))
            x = _proj_residual([y], [_bf(ssm_w_out[j])], x, gate, tm=1024, tn=512)
        shift, scale, gate = _mods(ffn_mod[i], bsz, d)
        x = _ffn(x, shift, scale, gate, _bf(ffn_w1[i]), _bf(ffn_w3[i]), _bf(ffn_w2[i]), final_w,
                 final_norm=(i == depth - 1), tm=512, tf=512)
    return x
```

```python
import functools

import jax
import jax.numpy as jnp
from jax import lax
from jax.experimental import pallas as pl
from jax.experimental.pallas import tpu as pltpu

F32 = jnp.float32
BF16 = jnp.bfloat16

EPS = 1e-5
RWKV_HEAD = 64
RWKV_LN_EPS = 64e-5
GDN_HEAD = 128
GDN_HEADS = 8
SSM_HEAD = 64
SSM_GROUPS = 8
SSM_STATE = 128
CONV_K = 4

LANES = 128
SUBLANES = 8
CHUNK = 64
VMEM_LIMIT = 56 * 1024 * 1024


def _cparams(sem):
    return pltpu.CompilerParams(dimension_semantics=sem, vmem_limit_bytes=VMEM_LIMIT)


def _bf(x):
    return x.astype(BF16)


def _dot(a, b):
    return jnp.dot(a, b, preferred_element_type=F32)


def _dot_nt(a, b):
    return lax.dot_general(a, b, (((1,), (1,)), ((), ())), preferred_element_type=F32)


def _dot_tn(a, b):
    return lax.dot_general(a, b, (((0,), (0,)), ((), ())), preferred_element_type=F32)


def _split2(x):
    hi = _bf(x)
    lo = _bf(x - hi.astype(F32))
    return hi, lo


def _split3(x):
    hi = _bf(x)
    r = x - hi.astype(F32)
    mid = _bf(r)
    lo = _bf(r - mid.astype(F32))
    return hi, mid, lo


def _dot3(a, b):
    ah, al = _split2(a)
    bh, bl = _split2(b)
    return _dot(ah, bh) + (_dot(ah, bl) + _dot(al, bh))


def _dot_sel_l(sel, x):
    hi, mid, lo = _split3(x)
    return _dot(sel, hi) + (_dot(sel, mid) + _dot(sel, lo))


def _dot_sel_r(x, sel):
    hi, mid, lo = _split3(x)
    return _dot(hi, sel) + (_dot(mid, sel) + _dot(lo, sel))


def _silu(x):
    return x * jax.nn.sigmoid(x)


def _softplus(x):
    return jnp.maximum(x, 0.0) + jnp.log(1.0 + jnp.exp(-jnp.abs(x)))


def _iota(shape, dim):
    return lax.broadcasted_iota(jnp.int32, shape, dim)


def _chunk_masks(n, chunk):
    ri = _iota((n, n), 0)
    ci = _iota((n, n), 1)
    same = (ri // chunk) == (ci // chunk)
    return same & (ri >= ci), same & (ri > ci)


INV_BASE = 8


def _unit_lower_inverse(a_strict, chunk):
    n = a_strict.shape[0]
    ri = _iota((n, n), 0)
    ci = _iota((n, n), 1)
    q = jnp.where((ri // INV_BASE) == (ci // INV_BASE), -a_strict, 0.0)
    t = (ri == ci).astype(F32) + q
    for _ in range(2):
        qb = _bf(q)
        q = _dot(qb, qb)
        yield
        t = t + _dot(_bf(t), _bf(q))
        yield
    s = INV_BASE
    while s < chunk:
        off = ((ri // (2 * s)) == (ci // (2 * s))) & ((ri // s) != (ci // s))
        tb = _bf(t)
        x = _dot(_bf(jnp.where(off, a_strict, 0.0)), tb)
        yield
        t = t - _dot(tb, _bf(x))
        yield
        s *= 2
    return t


def _interleave(gens):
    results = [None] * len(gens)
    live = list(enumerate(gens))
    while live:
        still = []
        for i, gen in live:
            try:
                next(gen)
                still.append((i, gen))
            except StopIteration as done:
                results[i] = done.value
        live = still
    return results


def _rows_bcast(x, rows, chunk):
    n = x.shape[0]
    parts = []
    for c0 in range(0, n, chunk):
        parts.append(jnp.broadcast_to(x[c0 + rows:c0 + rows + 1, :], (chunk, x.shape[1])))
    return jnp.concatenate(parts, axis=0) if len(parts) > 1 else parts[0]


def _adaln_kernel(c_ref, w_ref, b_ref, o_ref):
    o_ref[0] = _dot3(_silu(c_ref[...]), w_ref[0]) + b_ref[0]


def _adaln(c_pad, w, b):
    depth, d, n = w.shape
    tn = 1536
    assert n % tn == 0
    return pl.pallas_call(
        _adaln_kernel,
        grid=(depth, n // tn),
        in_specs=[
            pl.BlockSpec((SUBLANES, d), lambda l, j: (0, 0)),
            pl.BlockSpec((1, d, tn), lambda l, j: (l, 0, j)),
            pl.BlockSpec((1, 1, tn), lambda l, j: (l, 0, j)),
        ],
        out_specs=pl.BlockSpec((1, SUBLANES, tn), lambda l, j: (l, 0, j)),
        out_shape=jax.ShapeDtypeStruct((depth, SUBLANES, n), F32),
        compiler_params=_cparams(("parallel", "parallel")),
        name="adaln_mod",
    )(c_pad, w, b.reshape(depth, 1, n))


def _modulated_norm(x, shift, scale):
    ms = jnp.mean(x * x, axis=-1, keepdims=True)
    return (x * lax.rsqrt(ms + EPS)) * (1.0 + scale) + shift


def _norm_proj_kernel(x_ref, shift_ref, scale_ref, w_ref, o_ref, h_ref):
    @pl.when(pl.program_id(2) == 0)
    def _():
        h_ref[...] = _bf(_modulated_norm(x_ref[0], shift_ref[0], scale_ref[0]))

    o_ref[0] = _dot(h_ref[...], w_ref[...])


def _norm_proj(x, shift, scale, w_bf16, tm, tn):
    bsz, t, d = x.shape
    n = w_bf16.shape[1]
    assert t % tm == 0 and n % tn == 0
    return pl.pallas_call(
        _norm_proj_kernel,
        grid=(bsz, t // tm, n // tn),
        in_specs=[
            pl.BlockSpec((1, tm, d), lambda b, i, j: (b, i, 0)),
            pl.BlockSpec((1, 1, d), lambda b, i, j: (b, 0, 0)),
            pl.BlockSpec((1, 1, d), lambda b, i, j: (b, 0, 0)),
            pl.BlockSpec((d, tn), lambda b, i, j: (0, j)),
        ],
        out_specs=pl.BlockSpec((1, tm, tn), lambda b, i, j: (b, i, j)),
        out_shape=jax.ShapeDtypeStruct((bsz, t, n), F32),
        scratch_shapes=[pltpu.VMEM((tm, d), BF16)],
        compiler_params=_cparams(("parallel", "parallel", "arbitrary")),
        name="norm_proj",
    )(x, shift, scale, w_bf16)


def _proj_residual_kernel(n_in, *refs):
    y_refs = refs[:n_in]
    w_refs = refs[n_in:2 * n_in]
    x_ref, gate_ref, o_ref = refs[2 * n_in:]
    acc = _dot(y_refs[0][0], w_refs[0][...])
    for y_ref, w_ref in zip(y_refs[1:], w_refs[1:]):
        acc = acc + _dot(y_ref[0], w_ref[...])
    o_ref[0] = x_ref[0] + gate_ref[0] * acc


def _proj_residual(ys, ws, x, gate, tm, tn):
    bsz, t, d = x.shape
    n_in = len(ys)
    assert t % tm == 0 and d % tn == 0
    in_specs = [pl.BlockSpec((1, tm, y.shape[2]), lambda b, i, j: (b, i, 0)) for y in ys]
    in_specs += [pl.BlockSpec((w.shape[0], tn), lambda b, i, j: (0, j)) for w in ws]
    in_specs += [
        pl.BlockSpec((1, tm, tn), lambda b, i, j: (b, i, j)),
        pl.BlockSpec((1, 1, tn), lambda b, i, j: (b, 0, j)),
    ]
    return pl.pallas_call(
        functools.partial(_proj_residual_kernel, n_in),
        grid=(bsz, t // tm, d // tn),
        in_specs=in_specs,
        out_specs=pl.BlockSpec((1, tm, tn), lambda b, i, j: (b, i, j)),
        out_shape=jax.ShapeDtypeStruct((bsz, t, d), F32),
        compiler_params=_cparams(("parallel", "parallel", "arbitrary")),
        name="proj_residual",
    )(*ys, *ws, x, gate)


def _ffn_kernel(final_norm, x_ref, shift_ref, scale_ref, gate_ref, w1_ref, w3_ref, w2_ref, fw_ref, o_ref,
                h_ref, acc_ref):
    f = pl.program_id(2)

    @pl.when(f == 0)
    def _():
        h_ref[...] = _bf(_modulated_norm(x_ref[0], shift_ref[0], scale_ref[0]))
        acc_ref[...] = jnp.zeros_like(acc_ref)

    h = h_ref[...]
    a = _dot(h, w1_ref[...])
    b = _dot(h, w3_ref[...])
    acc_ref[...] += _dot(_bf(_silu(a) * b), w2_ref[...])

    @pl.when(f == pl.num_programs(2) - 1)
    def _():
        y = x_ref[0] + gate_ref[0] * acc_ref[...]
        if final_norm:
            ms = jnp.mean(y * y, axis=-1, keepdims=True)
            y = (y * lax.rsqrt(ms + EPS)) * fw_ref[...]
        o_ref[0] = y


def _ffn(x, shift, scale, gate, w1, w3, w2, final_w, final_norm, tm, tf):
    bsz, t, d = x.shape
    hidden = w1.shape[1]
    assert t % tm == 0 and hidden % tf == 0
    return pl.pallas_call(
        functools.partial(_ffn_kernel, final_norm),
        grid=(bsz, t // tm, hidden // tf),
        in_specs=[
            pl.BlockSpec((1, tm, d), lambda b, i, f: (b, i, 0)),
            pl.BlockSpec((1, 1, d), lambda b, i, f: (b, 0, 0)),
            pl.BlockSpec((1, 1, d), lambda b, i, f: (b, 0, 0)),
            pl.BlockSpec((1, 1, d), lambda b, i, f: (b, 0, 0)),
            pl.BlockSpec((d, tf), lambda b, i, f: (0, f)),
            pl.BlockSpec((d, tf), lambda b, i, f: (0, f)),
            pl.BlockSpec((tf, d), lambda b, i, f: (f, 0)),
            pl.BlockSpec((1, d), lambda b, i, f: (0, 0)),
        ],
        out_specs=pl.BlockSpec((1, tm, d), lambda b, i, f: (b, i, 0)),
        out_shape=jax.ShapeDtypeStruct((bsz, t, d), F32),
        scratch_shapes=[pltpu.VMEM((tm, d), BF16), pltpu.VMEM((tm, d), F32)],
        compiler_params=_cparams(("parallel", "parallel", "arbitrary")),
        name="swiglu_ffn",
    )(x, shift, scale, gate, w1, w3, w2, final_w)


RWKV_ROWS = 2 * CHUNK
RWKV_PAIRS = 4


def _head_half_sum(x, first_half):
    s0 = jnp.sum(jnp.where(first_half, x, 0.0), axis=-1, keepdims=True)
    s1 = jnp.sum(jnp.where(first_half, 0.0, x), axis=-1, keepdims=True)
    return jnp.where(first_half, s0, s1)


def _rwkv_kernel(pr_ref, pk_ref, pv_ref, pl_ref, mur_ref, muk_ref, muv_ref, mul_ref, w0_ref, a0_ref, kkw_ref,
                 kaw_ref, rkw_ref, lnw_ref, lnb_ref, w2_ref, a2_ref, g2_ref, o_ref, sh_ref, s_ref):
    ct = RWKV_ROWS
    wb = RWKV_PAIRS * LANES
    t_idx = pl.program_id(2)

    @pl.when(t_idx == 0)
    def _():
        sh_ref[pl.ds(0, SUBLANES), :] = jnp.zeros((SUBLANES, sh_ref.shape[1]), F32)
        s_ref[...] = jnp.zeros_like(s_ref)

    sh_ref[pl.ds(SUBLANES, ct), 0:wb] = pr_ref[0]
    sh_ref[pl.ds(SUBLANES, ct), wb:2 * wb] = pk_ref[0]
    sh_ref[pl.ds(SUBLANES, ct), 2 * wb:3 * wb] = pv_ref[0]
    sh_ref[pl.ds(SUBLANES, ct), 3 * wb:3 * wb + 256] = pl_ref[0]

    def lerp(lo, hi, mu_ref):
        cur = sh_ref[pl.ds(SUBLANES, ct), lo:hi]
        prev = sh_ref[pl.ds(SUBLANES - 1, ct), lo:hi]
        return cur + mu_ref[...] * (prev - cur)

    r = lerp(0, wb, mur_ref)
    k = lerp(wb, 2 * wb, muk_ref)
    v = lerp(2 * wb, 3 * wb, muv_ref)
    xl = lerp(3 * wb, 3 * wb + 256, mul_ref)
    sh_ref[pl.ds(0, SUBLANES), :] = sh_ref[pl.ds(ct, SUBLANES), :]
    pw = xl[:, 0:64]
    pa = xl[:, 64:128]
    pg = xl[:, 128:256]

    w = -_softplus(-(w0_ref[...] + _dot(_bf(jnp.tanh(pw)), w2_ref[...]))) - 0.5
    logd = -jnp.exp(w)
    a = jax.nn.sigmoid(a0_ref[...] + _dot(_bf(pa), a2_ref[...]))
    g = _dot(_bf(jax.nn.sigmoid(pg)), g2_ref[...])

    gens = []
    for p in range(RWKV_PAIRS):
        cols = slice(p * LANES, (p + 1) * LANES)
        gens.append(_rwkv_pair(
            r[:, cols], k[:, cols], v[:, cols], logd[:, cols], a[:, cols], g[:, cols], kkw_ref[:, cols],
            kaw_ref[:, cols], rkw_ref[:, cols], lnw_ref[:, cols], lnb_ref[:, cols], s_ref.at[p]))
    for p, out in enumerate(_interleave(gens)):
        o_ref[0, :, p * LANES:(p + 1) * LANES] = out


def _rwkv_pair(r, k, v, logd, a, g, kkw, kaw, rkw, lnw, lnb, s_ref):
    ct = RWKV_ROWS
    n_chunks = ct // CHUNK
    first_half = _iota((ct, LANES), 1) < RWKV_HEAD
    kk = k * kkw
    kk = kk * lax.rsqrt(_head_half_sum(kk * kk, first_half) + 1e-24)
    k2 = k * (1.0 + (a - 1.0) * kaw)
    b = kk * a

    incl_ct, _ = _chunk_masks(ct, CHUNK)
    c = _dot_sel_l(_bf(incl_ct.astype(F32)), logd)
    yield
    cm1 = c - logd
    cref = _rows_bcast(c, CHUNK // 2 - 1, CHUNK)
    cend = _rows_bcast(c, CHUNK - 1, CHUNK)
    e_neg = jnp.exp(cref - c)
    rt = r * jnp.exp(c - cref)
    kkt = kk * jnp.exp(cm1 - cref)
    bt = b * e_neg
    kt = k2 * e_neg
    kk0 = kk * jnp.exp(cm1)
    r0 = r * jnp.exp(c)
    e_end = jnp.exp(cend - c)
    bh = b * e_end
    kh = k2 * e_end
    dc = jnp.exp(cend)

    fh = _iota((CHUNK, LANES), 1) < RWKV_HEAD

    def stack(x, masked):
        parts = []
        for ci in range(n_chunks):
            xc = x[ci * CHUNK:(ci + 1) * CHUNK]
            if masked:
                parts += [jnp.where(fh, xc, 0.0), jnp.where(fh, 0.0, xc)]
            else:
                parts += [xc, xc]
        return jnp.concatenate(parts, axis=0)

    def unstack(x_st, ci):
        base = 2 * ci * CHUNK
        return x_st[base:base + CHUNK] + x_st[base + CHUNK:base + 2 * CHUNK]

    ns = 2 * ct
    lhs = _bf(jnp.concatenate([stack(kkt, True), stack(rt, True)], axis=0))
    rhs = _bf(jnp.concatenate([stack(bt, False), stack(kt, False)], axis=0))
    abig = _dot_nt(lhs, rhs)
    yield
    incl, strict = _chunk_masks(ns, CHUNK)
    a_ab = jnp.where(strict, abig[:ns, :ns], 0.0)
    a_ak = jnp.where(strict, abig[:ns, ns:], 0.0)
    a_rb = jnp.where(incl, abig[ns:, :ns], 0.0)
    a_rk = jnp.where(incl, abig[ns:, ns:], 0.0)

    v_st = _bf(stack(v, True))
    av = _dot(_bf(a_ak), v_st)
    ark = _dot(_bf(a_rk), v_st)
    tmat = yield from _unit_lower_inverse(a_ab, CHUNK)
    sol = _dot(_bf(tmat), _bf(jnp.concatenate([stack(kk0, True), av], axis=1)))
    yield
    ar = _dot(_bf(a_rb), _bf(sol))
    yield
    rm_st = stack(r0, True) - ar[:, :LANES]
    yv_st = ark - ar[:, LANES:]

    ri = _iota((LANES, LANES), 0)
    cj = _iota((LANES, LANES), 1)
    same_head = (ri // RWKV_HEAD) == (cj // RWKV_HEAD)

    ys = []
    for ci in range(n_chunks):
        rows = slice(ci * CHUNK, (ci + 1) * CHUNK)
        wm_c = unstack(sol[:, :LANES], ci)
        uv_c = unstack(sol[:, LANES:], ci)
        s_old = s_ref[...]
        s_bf = _bf(s_old)
        ys.append(_dot_nt(_bf(unstack(rm_st, ci)), s_bf) + unstack(yv_st, ci))
        m_t = jnp.where(same_head, _dot_tn(_bf(wm_c), _bf(bh[rows])), 0.0)
        hv_t = jnp.where(
            same_head,
            _dot_tn(_bf(jnp.concatenate([v[rows], -uv_c], axis=0)),
                    _bf(jnp.concatenate([kh[rows], bh[rows]], axis=0))),
            0.0)
        yield
        s_ref[...] = s_old * dc[ci * CHUNK:ci * CHUNK + 1, :] - _dot(s_bf, _bf(m_t)) + hv_t
        yield
    y = jnp.concatenate(ys, axis=0)

    inv_n = 1.0 / RWKV_HEAD
    mean = _head_half_sum(y, first_half) * inv_n
    yc = y - mean
    var = _head_half_sum(yc * yc, first_half) * inv_n
    yn = yc * lax.rsqrt(var + RWKV_LN_EPS) * lnw + lnb
    bonus = _head_half_sum(r * k2 * rkw, first_half) * v
    return _bf((yn + bonus) * g)


def _rwkv_mix(p, col0, params):
    bsz, t, _ = p.shape
    dim = params["w0"].shape[1]
    wb = RWKV_PAIRS * LANES
    nb = dim // wb
    ct = RWKV_ROWS
    assert t % ct == 0 and col0 % wb == 0 and (col0 + 3 * dim) % 256 == 0 and dim % wb == 0
    cb0 = col0 // wb
    lb0 = (col0 + 3 * dim) // 256

    def col_spec(off):
        return pl.BlockSpec((1, ct, wb), lambda b, h, i: (b, i, cb0 + off + h))

    def row_spec(off):
        return pl.BlockSpec((1, wb), lambda b, h, i: (0, off + h))

    mu = params["mu"]
    in_specs = [
        col_spec(0), col_spec(nb), col_spec(2 * nb),
        pl.BlockSpec((1, ct, 256), lambda b, h, i: (b, i, lb0)),
        row_spec(0), row_spec(nb), row_spec(2 * nb),
        pl.BlockSpec((1, 256), lambda b, h, i: (0, 3 * dim // 256)),
        row_spec(0), row_spec(0), row_spec(0), row_spec(0), row_spec(0), row_spec(0), row_spec(0),
        pl.BlockSpec((params["w2"].shape[0], wb), lambda b, h, i: (0, h)),
        pl.BlockSpec((params["a2"].shape[0], wb), lambda b, h, i: (0, h)),
        pl.BlockSpec((params["g2"].shape[0], wb), lambda b, h, i: (0, h)),
    ]
    return pl.pallas_call(
        _rwkv_kernel,
        grid=(bsz, nb, t // ct),
        in_specs=in_specs,
        out_specs=pl.BlockSpec((1, ct, wb), lambda b, h, i: (b, i, h)),
        out_shape=jax.ShapeDtypeStruct((bsz, t, dim), BF16),
        scratch_shapes=[pltpu.VMEM((ct + SUBLANES, 3 * wb + 256), F32),
                        pltpu.VMEM((RWKV_PAIRS, LANES, LANES), F32)],
        compiler_params=_cparams(("parallel", "parallel", "arbitrary")),
        name="rwkv7_mix",
    )(p, p, p, p, mu, mu, mu, mu, params["w0"], params["a0"], params["k_k"], params["k_a"], params["r_k"],
      params["ln_w"], params["ln_b"], params["w2"], params["a2"], params["g2"])


GDN_ROWS = 4 * CHUNK
GDN_HEADS_PER_STEP = 4


def _causal_conv(sh_ref, lo, hi, w, n_rows):
    acc = None
    for j in range(CONV_K):
        term = sh_ref[pl.ds(SUBLANES - (CONV_K - 1) + j, n_rows), lo:hi] * w[j:j + 1, :]
        acc = term if acc is None else acc + term
    return acc


def _gdn_kernel(q_ref, k_ref, v_ref, z_ref, gt_ref, cwq_ref, cwk_ref, cwv_ref, alog_ref, dtb_ref, nw_ref,
                o_ref, sh_ref, s_ref):
    ct = GDN_ROWS
    wb = GDN_HEADS_PER_STEP * LANES
    t_idx = pl.program_id(2)

    @pl.when(t_idx == 0)
    def _():
        sh_ref[pl.ds(0, SUBLANES), :] = jnp.zeros((SUBLANES, sh_ref.shape[1]), F32)
        s_ref[...] = jnp.zeros_like(s_ref)

    sh_ref[pl.ds(SUBLANES, ct), 0:wb] = q_ref[0]
    sh_ref[pl.ds(SUBLANES, ct), wb:2 * wb] = k_ref[0]
    sh_ref[pl.ds(SUBLANES, ct), 2 * wb:3 * wb] = v_ref[0]
    q = _silu(_causal_conv(sh_ref, 0, wb, cwq_ref[...], ct))
    k = _silu(_causal_conv(sh_ref, wb, 2 * wb, cwk_ref[...], ct))
    v = _silu(_causal_conv(sh_ref, 2 * wb, 3 * wb, cwv_ref[...], ct))
    sh_ref[pl.ds(0, SUBLANES), :] = sh_ref[pl.ds(ct, SUBLANES), :]

    gates = gt_ref[0]
    beta_all = jax.nn.sigmoid(gates)
    g_all = -jnp.exp(alog_ref[...]) * _softplus(gates + dtb_ref[...])
    z = z_ref[0]
    gens = []
    for p in range(GDN_HEADS_PER_STEP):
        cols = slice(p * LANES, (p + 1) * LANES)
        h_idx = pl.program_id(1) * GDN_HEADS_PER_STEP + p
        gens.append(_gdn_head(q[:, cols], k[:, cols], v[:, cols], z[:, cols], beta_all, g_all, h_idx,
                              nw_ref[...], s_ref.at[p]))
    for p, out in enumerate(_interleave(gens)):
        o_ref[0, :, p * LANES:(p + 1) * LANES] = out


def _gdn_head(q, k, v, z, beta_all, g_all, h_idx, norm_w, s_ref):
    ct = GDN_ROWS
    n_chunks = ct // CHUNK
    q = q * lax.rsqrt(jnp.sum(q * q, axis=-1, keepdims=True) + 1e-6) * (GDN_HEAD ** -0.5)
    k = k * lax.rsqrt(jnp.sum(k * k, axis=-1, keepdims=True) + 1e-6)
    lane = _iota((ct, LANES), 1)
    beta = jnp.sum(jnp.where(lane == h_idx, beta_all, 0.0), axis=-1, keepdims=True)
    g = jnp.sum(jnp.where(lane == h_idx + GDN_HEADS, g_all, 0.0), axis=-1, keepdims=True)

    incl, strict = _chunk_masks(ct, CHUNK)
    gc = _dot_sel_l(_bf(incl.astype(F32)), jnp.broadcast_to(g, (ct, LANES)))
    kb = k * beta
    kq = _dot_nt(_bf(jnp.concatenate([kb, q], axis=0)), _bf(k))
    yield
    gcw = jnp.concatenate([gc] * (ct // LANES), axis=1)
    decay = jnp.where(incl, jnp.exp(jnp.minimum(gcw - gcw.T, 0.0)), 0.0)
    a_mat = jnp.where(strict, kq[:ct] * decay, 0.0)
    qk = kq[ct:] * decay
    tmat = yield from _unit_lower_inverse(a_mat, CHUNK)

    egc = jnp.exp(gc)
    sol = _dot(_bf(tmat), _bf(jnp.concatenate([v * beta, kb * egc], axis=1)))
    yield
    qu = _dot(_bf(qk), _bf(sol))
    yield
    o_v = qu[:, :LANES]
    o_m = q * egc - qu[:, LANES:]
    gl = _rows_bcast(gc, CHUNK - 1, CHUNK)
    kd = k * jnp.exp(gl - gc)
    egl = jnp.exp(gl)

    outs = []
    for ci in range(n_chunks):
        rows = slice(ci * CHUNK, (ci + 1) * CHUNK)
        s_old = s_ref[...]
        s_bf = _bf(s_old)
        outs.append(_dot(_bf(o_m[rows]), s_bf) + o_v[rows])
        hm = _dot_tn(_bf(kd[rows]), _bf(sol[rows]))
        yield
        s_ref[...] = s_old * egl[ci * CHUNK:ci * CHUNK + 1, :] - _dot(_bf(hm[:, LANES:]), s_bf) + hm[:, :LANES]
        yield
    o = jnp.concatenate(outs, axis=0)

    ms = jnp.mean(o * o, axis=-1, keepdims=True)
    return _bf((o * lax.rsqrt(ms + EPS)) * norm_w * _silu(z))


def _gdn_mix(p, col0, gate_col, params):
    bsz, t, _ = p.shape
    ct = GDN_ROWS
    wb = GDN_HEADS_PER_STEP * LANES
    dim = GDN_HEADS * GDN_HEAD
    nb = dim // wb
    assert t % ct == 0 and col0 % wb == 0 and dim % wb == 0 and gate_col % LANES == 0
    cb0 = col0 // wb
    gate_blk = gate_col // LANES

    def col_spec(off):
        return pl.BlockSpec((1, ct, wb), lambda b, h, i: (b, i, cb0 + off + h))

    def cw_spec(off):
        return pl.BlockSpec((CONV_K, wb), lambda b, h, i: (0, off + h))

    full_row = pl.BlockSpec((1, LANES), lambda b, h, i: (0, 0))
    cw = params["conv_w"]
    return pl.pallas_call(
        _gdn_kernel,
        grid=(bsz, nb, t // ct),
        in_specs=[
            col_spec(0), col_spec(nb), col_spec(2 * nb), col_spec(3 * nb),
            pl.BlockSpec((1, ct, LANES), lambda b, h, i: (b, i, gate_blk)),
            cw_spec(0), cw_spec(nb), cw_spec(2 * nb),
            full_row, full_row, full_row,
        ],
        out_specs=pl.BlockSpec((1, ct, wb), lambda b, h, i: (b, i, h)),
        out_shape=jax.ShapeDtypeStruct((bsz, t, dim), BF16),
        scratch_shapes=[pltpu.VMEM((ct + SUBLANES, 3 * wb), F32),
                        pltpu.VMEM((GDN_HEADS_PER_STEP, GDN_HEAD, GDN_HEAD), F32)],
        compiler_params=_cparams(("parallel", "parallel", "arbitrary")),
        name="gated_deltanet_mix",
    )(p, p, p, p, p, cw, cw, cw, params["a_log_row"], params["dt_bias_row"], params["norm_w"])


SSD_CHUNK = 128
SSD_GROUP_CH = 512


def _ssd_kernel(z_ref, x_ref, b_ref, c_ref, dt_ref, cwx_ref, cwb_ref, cwc_ref, cbx_ref, cbb_ref, cbc_ref,
                dtb_ref, alog_ref, alogs_ref, dsk_ref, nw_ref, o_ref, sh_ref, at_ref, st_ref):
    ct = SSD_CHUNK
    gch = SSD_GROUP_CH
    hpg = gch // SSM_HEAD
    g_idx = pl.program_id(1)
    t_idx = pl.program_id(2)

    @pl.when(t_idx == 0)
    def _():
        sh_ref[pl.ds(0, SUBLANES), :] = jnp.zeros((SUBLANES, sh_ref.shape[1]), F32)
        st_ref[...] = jnp.zeros_like(st_ref)

    sh_ref[pl.ds(SUBLANES, ct), 0:gch] = x_ref[0]
    sh_ref[pl.ds(SUBLANES, ct), gch:gch + LANES] = b_ref[0]
    sh_ref[pl.ds(SUBLANES, ct), gch + LANES:gch + 2 * LANES] = c_ref[0]
    xs = _silu(_causal_conv(sh_ref, 0, gch, cwx_ref[...], ct) + cbx_ref[...])
    bm = _silu(_causal_conv(sh_ref, gch, gch + LANES, cwb_ref[...], ct) + cbb_ref[...])
    cm = _silu(_causal_conv(sh_ref, gch + LANES, gch + 2 * LANES, cwc_ref[...], ct) + cbc_ref[...])
    sh_ref[pl.ds(0, SUBLANES), :] = sh_ref[pl.ds(ct, SUBLANES), :]

    dtv = _softplus(dt_ref[0] + dtb_ref[...])
    expand = _bf((_iota((LANES, gch), 0) == g_idx * hpg + _iota((LANES, gch), 1) // SSM_HEAD).astype(F32))
    dt_exp = _dot_sel_r(dtv, expand)
    a_exp = dt_exp * (-jnp.exp(alog_ref[...]))
    xdt = xs * dt_exp

    ri = _iota((ct, ct), 0)
    ci = _iota((ct, ct), 1)
    incl = ri >= ci
    acum = _dot_sel_l(_bf(incl.astype(F32)), a_exp)
    at_ref[...] = (dtv * (-jnp.exp(alogs_ref[...]))).T
    a_rows = at_ref[pl.ds(pl.multiple_of(g_idx * hpg, hpg), hpg), :]
    acum_rows = _dot_sel_r(a_rows, _bf((ri <= ci).astype(F32)))

    cb = _dot_nt(_bf(cm), _bf(bm))
    st_old = st_ref[...]
    y_state = _dot(_bf(cm), _bf(st_old)) * jnp.exp(acum)

    first_half = _iota((ct, LANES), 1) < SSM_HEAD
    y_parts = []
    for jp in range(hpg // 2):
        x_pair = _bf(xdt[:, jp * LANES:(jp + 1) * LANES])
        pair = []
        for j in (2 * jp, 2 * jp + 1):
            col = jnp.broadcast_to(acum[:, j * SSM_HEAD:j * SSM_HEAD + 1], (ct, ct))
            row = jnp.broadcast_to(acum_rows[j:j + 1, :], (ct, ct))
            lmat = jnp.where(incl, jnp.exp(jnp.minimum(col - row, 0.0)), 0.0)
            pair.append(_dot(_bf(cb * lmat), x_pair))
        y_parts.append(jnp.where(first_half, pair[0], pair[1]))
    y = jnp.concatenate(y_parts, axis=1) + y_state + xs * dsk_ref[...]

    a_last = acum[ct - 1:ct, :]
    st_ref[...] = st_old * jnp.exp(a_last) + _dot_tn(_bf(bm), _bf(xdt * jnp.exp(a_last - acum)))

    yg = y * _silu(z_ref[0])
    ms = jnp.mean(yg * yg, axis=-1, keepdims=True)
    o_ref[0] = _bf((yg * lax.rsqrt(ms + EPS)) * nw_ref[...])


def _ssd_mix(zx, params):
    bsz, t, _ = zx.shape
    ct = SSD_CHUNK
    gch = SSD_GROUP_CH
    d_inner = params["norm_w"].shape[1]
    ng = d_inner // gch
    assert t % ct == 0 and ng == SSM_GROUPS
    xb = d_inner // gch
    bb = 2 * d_inner // LANES
    cbk = bb + ng
    dtb = cbk + ng

    cw = params["conv_w"]
    cbias = params["conv_b"]
    wide = lambda off: pl.BlockSpec((1, ct, gch), lambda b, g, i: (b, i, off + g))
    narrow = lambda off: pl.BlockSpec((1, ct, LANES), lambda b, g, i: (b, i, off + g))
    row_wide = pl.BlockSpec((1, gch), lambda b, g, i: (0, g))
    row_full = pl.BlockSpec((1, LANES), lambda b, g, i: (0, 0))
    cb_b = d_inner // LANES
    return pl.pallas_call(
        _ssd_kernel,
        grid=(bsz, ng, t // ct),
        in_specs=[
            wide(0), wide(xb), narrow(bb), narrow(cbk),
            pl.BlockSpec((1, ct, LANES), lambda b, g, i: (b, i, dtb)),
            pl.BlockSpec((CONV_K, gch), lambda b, g, i: (0, g)),
            pl.BlockSpec((CONV_K, LANES), lambda b, g, i: (0, cb_b + g)),
            pl.BlockSpec((CONV_K, LANES), lambda b, g, i: (0, cb_b + ng + g)),
            pl.BlockSpec((1, gch), lambda b, g, i: (0, g)),
            pl.BlockSpec((1, LANES), lambda b, g, i: (0, cb_b + g)),
            pl.BlockSpec((1, LANES), lambda b, g, i: (0, cb_b + ng + g)),
            row_full, row_wide, row_full, row_wide, row_wide,
        ],
        out_specs=pl.BlockSpec((1, ct, gch), lambda b, g, i: (b, i, g)),
        out_shape=jax.ShapeDtypeStruct((bsz, t, d_inner), BF16),
        scratch_shapes=[
            pltpu.VMEM((ct + SUBLANES, gch + 2 * LANES), F32),
            pltpu.VMEM((LANES, ct), F32),
            pltpu.VMEM((SSM_STATE, gch), F32),
        ],
        compiler_params=_cparams(("parallel", "parallel", "arbitrary")),
        name="mamba2_ssd_mix",
    )(zx, zx, zx, zx, zx, cw, cw, cw, cbias, cbias, cbias, params["dt_bias_row"], params["a_log_exp"],
      params["a_log_row"], params["d_exp"], params["norm_w"])


def _pad_cols(w, n):
    return jnp.pad(w, ((0, 0), (0, n - w.shape[1])))


def _lane_row(v, offset):
    return jnp.pad(v, (offset, LANES - offset - v.shape[0])).reshape(1, LANES)


def _mods(mod, bsz, d):
    mod = mod[:bsz]
    return mod[:, None, 0:d], mod[:, None, d:2 * d], mod[:, None, 2 * d:3 * d]


def kernel(x, c, ada_mix_w, ada_mix_b, ada_ffn_w, ada_ffn_b, hg_w_in, hg_w_out, rwkv_mu, rwkv_w0, rwkv_w2,
           rwkv_a0, rwkv_a2, rwkv_g2, rwkv_k_k, rwkv_k_a, rwkv_r_k, rwkv_ln_w, rwkv_ln_b, gdn_conv_w,
           gdn_a_log, gdn_dt_bias, gdn_norm_w, ssm_w_in, ssm_conv_w, ssm_conv_b, ssm_dt_bias, ssm_a_log,
           ssm_d, ssm_norm_w, ssm_w_out, ffn_w1, ffn_w3, ffn_w2, final_norm_w):
    bsz, t, d = x.shape
    depth = ada_mix_w.shape[0]
    assert bsz <= SUBLANES

    c_pad = jnp.pad(c, ((0, SUBLANES - bsz), (0, 0)))
    mix_mod = _adaln(c_pad, ada_mix_w, ada_mix_b)
    ffn_mod = _adaln(c_pad, ada_ffn_w, ada_ffn_b)
    final_w = final_norm_w.reshape(1, d)

    for i in range(depth):
        j = i // 2
        shift, scale, gate = _mods(mix_mod[i], bsz, d)
        if i % 2 == 0:
            rwkv_dim = rwkv_w0.shape[1]
            gdn_dim = GDN_HEADS * GDN_HEAD
            rwkv_cols = 3 * rwkv_dim + rwkv_w2.shape[1] + rwkv_a2.shape[1] + rwkv_g2.shape[1]
            w_in = hg_w_in[j]
            w_in = jnp.concatenate(
                [w_in[:, rwkv_cols:rwkv_cols + 4 * gdn_dim], w_in[:, :rwkv_cols], w_in[:, rwkv_cols + 4 * gdn_dim:]],
                axis=1)
            p = _norm_proj(x, shift, scale, _bf(_pad_cols(w_in, 7680)), tm=1024, tn=1280)
            row = lambda v: v.reshape(1, -1)
            y_a = _rwkv_mix(p, 4 * gdn_dim, dict(
                mu=row(rwkv_mu[j]), w0=row(rwkv_w0[j]), a0=row(rwkv_a0[j]), k_k=row(rwkv_k_k[j]),
                k_a=row(rwkv_k_a[j]), r_k=row(rwkv_r_k[j]), ln_w=row(rwkv_ln_w[j]), ln_b=row(rwkv_ln_b[j]),
                w2=_bf(rwkv_w2[j]), a2=_bf(rwkv_a2[j]), g2=_bf(rwkv_g2[j])))
            y_b = _gdn_mix(p, 0, 4 * gdn_dim + rwkv_cols, dict(
                conv_w=gdn_conv_w[j], a_log_row=_lane_row(gdn_a_log[j], GDN_HEADS),
                dt_bias_row=_lane_row(gdn_dt_bias[j], GDN_HEADS), norm_w=row(gdn_norm_w[j])))
            w_out = _bf(hg_w_out[j])
            x = _proj_residual([y_a, y_b], [w_out[:rwkv_dim], w_out[rwkv_dim:rwkv_dim + gdn_dim]], x, gate,
                               tm=1024, tn=512)
        else:
            d_inner = ssm_norm_w.shape[1]
            zx = _norm_proj(x, shift, scale, _bf(_pad_cols(ssm_w_in[j], 10752)), tm=1024, tn=1536)
            y = _ssd_mix(zx, dict(
                conv_w=ssm_conv_w[j], conv_b=ssm_conv_b[j].reshape(1, -1),
                dt_bias_row=_lane_row(ssm_dt_bias[j], 0), a_log_exp=jnp.repeat(ssm_a_log[j], SSM_HEAD).reshape(1, -1),
                a_log_row=_lane_row(ssm_a_log[j], 0), d_exp=jnp.repeat(ssm_d[j], SSM_HEAD).reshape(1, -1),
                norm_w=ssm_norm_w[j].reshape(1, d_inner)))
            x = _proj_residual([y], [_bf(ssm_w_out[j])], x, gate, tm=1024, tn=512)
        shift, scale, gate = _mods(ffn_mod[i], bsz, d)
        x = _ffn(x, shift, scale, gate, _bf(ffn_w1[i]), _bf(ffn_w3[i]), _bf(ffn_w2[i]), final_w,
                 final_norm=(i == depth - 1), tm=512, tf=512)
    return x
```

```python
import functools

import jax
import jax.numpy as jnp
from jax import lax
from jax.experimental import pallas as pl
from jax.experimental.pallas import tpu as pltpu

F32 = jnp.float32
BF16 = jnp.bfloat16

EPS = 1e-5
RWKV_HEAD = 64
RWKV_LN_EPS = 64e-5
GDN_HEAD = 128
GDN_HEADS = 8
SSM_HEAD = 64
SSM_GROUPS = 8
SSM_STATE = 128
CONV_K = 4

LANES = 128
SUBLANES = 8
CHUNK = 64
VMEM_LIMIT = 56 * 1024 * 1024


def _cparams(sem):
    return pltpu.CompilerParams(dimension_semantics=sem, vmem_limit_bytes=VMEM_LIMIT)


def _bf(x):
    return x.astype(BF16)


def _dot(a, b):
    return jnp.dot(a, b, preferred_element_type=F32)


def _dot_nt(a, b):
    return lax.dot_general(a, b, (((1,), (1,)), ((), ())), preferred_element_type=F32)


def _dot_tn(a, b):
    return lax.dot_general(a, b, (((0,), (0,)), ((), ())), preferred_element_type=F32)


def _split2(x):
    hi = _bf(x)
    lo = _bf(x - hi.astype(F32))
    return hi, lo


def _split3(x):
    hi = _bf(x)
    r = x - hi.astype(F32)
    mid = _bf(r)
    lo = _bf(r - mid.astype(F32))
    return hi, mid, lo


def _dot3(a, b):
    ah, al = _split2(a)
    bh, bl = _split2(b)
    return _dot(ah, bh) + (_dot(ah, bl) + _dot(al, bh))


def _dot_sel_l(sel, x):
    hi, mid, lo = _split3(x)
    return _dot(sel, hi) + (_dot(sel, mid) + _dot(sel, lo))


def _dot_sel_r(x, sel):
    hi, mid, lo = _split3(x)
    return _dot(hi, sel) + (_dot(mid, sel) + _dot(lo, sel))


def _silu(x):
    h = 0.5 * x
    return h + h * jnp.tanh(h)


def _softplus(x):
    return jnp.maximum(x, 0.0) + jnp.log(1.0 + jnp.exp(-jnp.abs(x)))


def _iota(shape, dim):
    return lax.broadcasted_iota(jnp.int32, shape, dim)


def _chunk_masks(n, chunk):
    ri = _iota((n, n), 0)
    ci = _iota((n, n), 1)
    same = (ri // chunk) == (ci // chunk)
    return same & (ri >= ci), same & (ri > ci)


INV_BASE = 8


def _unit_lower_inverse(a_strict, chunk):
    n = a_strict.shape[0]
    ri = _iota((n, n), 0)
    ci = _iota((n, n), 1)
    q = jnp.where((ri // INV_BASE) == (ci // INV_BASE), -a_strict, 0.0)
    t = (ri == ci).astype(F32) + q
    for _ in range(2):
        qb = _bf(q)
        q = _dot(qb, qb)
        yield
        t = t + _dot(_bf(t), _bf(q))
        yield
    s = INV_BASE
    while s < chunk:
        off = ((ri // (2 * s)) == (ci // (2 * s))) & ((ri // s) != (ci // s))
        tb = _bf(t)
        x = _dot(_bf(jnp.where(off, a_strict, 0.0)), tb)
        yield
        t = t - _dot(tb, _bf(x))
        yield
        s *= 2
    return t


def _interleave(gens):
    results = [None] * len(gens)
    live = list(enumerate(gens))
    while live:
        still = []
        for i, gen in live:
            try:
                next(gen)
                still.append((i, gen))
            except StopIteration as done:
                results[i] = done.value
        live = still
    return results


def _rows_bcast(x, rows, chunk):
    n = x.shape[0]
    parts = []
    for c0 in range(0, n, chunk):
        parts.append(jnp.broadcast_to(x[c0 + rows:c0 + rows + 1, :], (chunk, x.shape[1])))
    return jnp.concatenate(parts, axis=0) if len(parts) > 1 else parts[0]


def _adaln_kernel(c_ref, w_ref, b_ref, o_ref):
    o_ref[0] = _dot3(_silu(c_ref[...]), w_ref[0]) + b_ref[0]


def _adaln(c_pad, w, b):
    depth, d, n = w.shape
    tn = 1536
    assert n % tn == 0
    return pl.pallas_call(
        _adaln_kernel,
        grid=(depth, n // tn),
        in_specs=[
            pl.BlockSpec((SUBLANES, d), lambda l, j: (0, 0)),
            pl.BlockSpec((1, d, tn), lambda l, j: (l, 0, j)),
            pl.BlockSpec((1, 1, tn), lambda l, j: (l, 0, j)),
        ],
        out_specs=pl.BlockSpec((1, SUBLANES, tn), lambda l, j: (l, 0, j)),
        out_shape=jax.ShapeDtypeStruct((depth, SUBLANES, n), F32),
        compiler_params=_cparams(("parallel", "parallel")),
        name="adaln_mod",
    )(c_pad, w, b.reshape(depth, 1, n))


def _modulated_norm(x, shift, scale):
    ms = jnp.mean(x * x, axis=-1, keepdims=True)
    return (x * lax.rsqrt(ms + EPS)) * (1.0 + scale) + shift


def _norm_proj_kernel(x_ref, shift_ref, scale_ref, w_ref, o_ref, h_ref):
    @pl.when(pl.program_id(2) == 0)
    def _():
        h_ref[...] = _bf(_modulated_norm(x_ref[0], shift_ref[0], scale_ref[0]))

    o_ref[0] = _dot(h_ref[...], w_ref[...])


def _norm_proj(x, shift, scale, w_bf16, tm, tn):
    bsz, t, d = x.shape
    n = w_bf16.shape[1]
    assert t % tm == 0 and n % tn == 0
    return pl.pallas_call(
        _norm_proj_kernel,
        grid=(bsz, t // tm, n // tn),
        in_specs=[
            pl.BlockSpec((1, tm, d), lambda b, i, j: (b, i, 0)),
            pl.BlockSpec((1, 1, d), lambda b, i, j: (b, 0, 0)),
            pl.BlockSpec((1, 1, d), lambda b, i, j: (b, 0, 0)),
            pl.BlockSpec((d, tn), lambda b, i, j: (0, j)),
        ],
        out_specs=pl.BlockSpec((1, tm, tn), lambda b, i, j: (b, i, j)),
        out_shape=jax.ShapeDtypeStruct((bsz, t, n), F32),
        scratch_shapes=[pltpu.VMEM((tm, d), BF16)],
        compiler_params=_cparams(("parallel", "parallel", "arbitrary")),
        name="norm_proj",
    )(x, shift, scale, w_bf16)


def _proj_residual_kernel(n_in, *refs):
    y_refs = refs[:n_in]
    w_refs = refs[n_in:2 * n_in]
    x_ref, gate_ref, o_ref = refs[2 * n_in:]
    acc = _dot(y_refs[0][0], w_refs[0][...])
    for y_ref, w_ref in zip(y_refs[1:], w_refs[1:]):
        acc = acc + _dot(y_ref[0], w_ref[...])
    o_ref[0] = x_ref[0] + gate_ref[0] * acc


def _proj_residual(ys, ws, x, gate, tm, tn):
    bsz, t, d = x.shape
    n_in = len(ys)
    assert t % tm == 0 and d % tn == 0
    in_specs = [pl.BlockSpec((1, tm, y.shape[2]), lambda b, i, j: (b, i, 0)) for y in ys]
    in_specs += [pl.BlockSpec((w.shape[0], tn), lambda b, i, j: (0, j)) for w in ws]
    in_specs += [
        pl.BlockSpec((1, tm, tn), lambda b, i, j: (b, i, j)),
        pl.BlockSpec((1, 1, tn), lambda b, i, j: (b, 0, j)),
    ]
    return pl.pallas_call(
        functools.partial(_proj_residual_kernel, n_in),
        grid=(bsz, t // tm, d // tn),
        in_specs=in_specs,
        out_specs=pl.BlockSpec((1, tm, tn), lambda b, i, j: (b, i, j)),
        out_shape=jax.ShapeDtypeStruct((bsz, t, d), F32),
        compiler_params=_cparams(("parallel", "parallel", "arbitrary")),
        name="proj_residual",
    )(*ys, *ws, x, gate)


def _ffn_kernel(final_norm, x_ref, shift_ref, scale_ref, gate_ref, w1_ref, w3_ref, w2_ref, fw_ref, o_ref,
                h_ref):
    f = pl.program_id(2)

    @pl.when(f == 0)
    def _():
        h_ref[...] = _bf(_modulated_norm(x_ref[0], shift_ref[0], scale_ref[0]))
        o_ref[...] = jnp.zeros_like(o_ref)

    h = h_ref[...]
    a = _dot(h, w1_ref[...])
    b = _dot(h, w3_ref[...])
    o_ref[0] += _dot(_bf(_silu(a) * b), w2_ref[...])

    @pl.when(f == pl.num_programs(2) - 1)
    def _():
        y = x_ref[0] + gate_ref[0] * o_ref[0]
        if final_norm:
            ms = jnp.mean(y * y, axis=-1, keepdims=True)
            y = (y * lax.rsqrt(ms + EPS)) * fw_ref[...]
        o_ref[0] = y


def _ffn(x, shift, scale, gate, w1, w3, w2, layer, final_w, final_norm, tm, tf):
    bsz, t, d = x.shape
    hidden = w1.shape[2]
    assert t % tm == 0 and hidden % tf == 0
    return pl.pallas_call(
        functools.partial(_ffn_kernel, final_norm),
        grid=(bsz, t // tm, hidden // tf),
        in_specs=[
            pl.BlockSpec((1, tm, d), lambda b, i, f: (b, i, 0)),
            pl.BlockSpec((1, 1, d), lambda b, i, f: (b, 0, 0)),
            pl.BlockSpec((1, 1, d), lambda b, i, f: (b, 0, 0)),
            pl.BlockSpec((1, 1, d), lambda b, i, f: (b, 0, 0)),
            pl.BlockSpec((None, d, tf), lambda b, i, f: (layer, 0, f)),
            pl.BlockSpec((None, d, tf), lambda b, i, f: (layer, 0, f)),
            pl.BlockSpec((None, tf, d), lambda b, i, f: (layer, f, 0)),
            pl.BlockSpec((1, d), lambda b, i, f: (0, 0)),
        ],
        out_specs=pl.BlockSpec((1, tm, d), lambda b, i, f: (b, i, 0)),
        out_shape=jax.ShapeDtypeStruct((bsz, t, d), F32),
        scratch_shapes=[pltpu.VMEM((tm, d), BF16)],
        compiler_params=_cparams(("parallel", "parallel", "arbitrary")),
        name="swiglu_ffn",
    )(x, shift, scale, gate, w1, w3, w2, final_w)


RWKV_ROWS = 2 * CHUNK
RWKV_PAIRS = 4


def _head_half_sum(x, first_half):
    s0 = jnp.sum(jnp.where(first_half, x, 0.0), axis=-1, keepdims=True)
    s1 = jnp.sum(jnp.where(first_half, 0.0, x), axis=-1, keepdims=True)
    return jnp.where(first_half, s0, s1)


def _rwkv_kernel(pr_ref, pk_ref, pv_ref, pl_ref, mur_ref, muk_ref, muv_ref, mul_ref, w0_ref, a0_ref, kkw_ref,
                 kaw_ref, rkw_ref, lnw_ref, lnb_ref, w2_ref, a2_ref, g2_ref, o_ref, sh_ref, s_ref):
    ct = RWKV_ROWS
    wb = RWKV_PAIRS * LANES
    t_idx = pl.program_id(2)

    @pl.when(t_idx == 0)
    def _():
        sh_ref[pl.ds(0, SUBLANES), :] = jnp.zeros((SUBLANES, sh_ref.shape[1]), F32)
        s_ref[...] = jnp.zeros_like(s_ref)

    sh_ref[pl.ds(SUBLANES, ct), 0:wb] = pr_ref[0]
    sh_ref[pl.ds(SUBLANES, ct), wb:2 * wb] = pk_ref[0]
    sh_ref[pl.ds(SUBLANES, ct), 2 * wb:3 * wb] = pv_ref[0]
    sh_ref[pl.ds(SUBLANES, ct), 3 * wb:3 * wb + 256] = pl_ref[0]

    def lerp(lo, hi, mu_ref):
        cur = sh_ref[pl.ds(SUBLANES, ct), lo:hi]
        prev = sh_ref[pl.ds(SUBLANES - 1, ct), lo:hi]
        return cur + mu_ref[...] * (prev - cur)

    r = lerp(0, wb, mur_ref)
    k = lerp(wb, 2 * wb, muk_ref)
    v = lerp(2 * wb, 3 * wb, muv_ref)
    xl = lerp(3 * wb, 3 * wb + 256, mul_ref)
    sh_ref[pl.ds(0, SUBLANES), :] = sh_ref[pl.ds(ct, SUBLANES), :]
    pw = xl[:, 0:64]
    pa = xl[:, 64:128]
    pg = xl[:, 128:256]

    w = -_softplus(-(w0_ref[...] + _dot(_bf(jnp.tanh(pw)), w2_ref[...]))) - 0.5
    logd = -jnp.exp(w)
    a = jax.nn.sigmoid(a0_ref[...] + _dot(_bf(pa), a2_ref[...]))
    g = _dot(_bf(jax.nn.sigmoid(pg)), g2_ref[...])

    gens = []
    for p in range(RWKV_PAIRS):
        cols = slice(p * LANES, (p + 1) * LANES)
        gens.append(_rwkv_pair(
            r[:, cols], k[:, cols], v[:, cols], logd[:, cols], a[:, cols], g[:, cols], kkw_ref[:, cols],
            kaw_ref[:, cols], rkw_ref[:, cols], lnw_ref[:, cols], lnb_ref[:, cols], s_ref.at[p]))
    for p, out in enumerate(_interleave(gens)):
        o_ref[0, :, p * LANES:(p + 1) * LANES] = out


def _rwkv_pair(r, k, v, logd, a, g, kkw, kaw, rkw, lnw, lnb, s_ref):
    ct = RWKV_ROWS
    n_chunks = ct // CHUNK
    first_half = _iota((ct, LANES), 1) < RWKV_HEAD
    kk = k * kkw
    kk = kk * lax.rsqrt(_head_half_sum(kk * kk, first_half) + 1e-24)
    k2 = k * (1.0 + (a - 1.0) * kaw)
    b = kk * a

    incl_ct, _ = _chunk_masks(ct, CHUNK)
    c = _dot_sel_l(_bf(incl_ct.astype(F32)), logd)
    yield
    cm1 = c - logd
    cref = _rows_bcast(c, CHUNK // 2 - 1, CHUNK)
    cend = _rows_bcast(c, CHUNK - 1, CHUNK)
    e_neg = jnp.exp(cref - c)
    rt = r * jnp.exp(c - cref)
    kkt = kk * jnp.exp(cm1 - cref)
    bt = b * e_neg
    kt = k2 * e_neg
    kk0 = kk * jnp.exp(cm1)
    r0 = r * jnp.exp(c)
    e_end = jnp.exp(cend - c)
    bh = b * e_end
    kh = k2 * e_end
    dc = jnp.exp(cend)

    fh = _iota((CHUNK, LANES), 1) < RWKV_HEAD

    def stack(x, masked):
        parts = []
        for ci in range(n_chunks):
            xc = x[ci * CHUNK:(ci + 1) * CHUNK]
            if masked:
                parts += [jnp.where(fh, xc, 0.0), jnp.where(fh, 0.0, xc)]
            else:
                parts += [xc, xc]
        return jnp.concatenate(parts, axis=0)

    def unstack(x_st, ci):
        base = 2 * ci * CHUNK
        return x_st[base:base + CHUNK] + x_st[base + CHUNK:base + 2 * CHUNK]

    ns = 2 * ct
    lhs = _bf(jnp.concatenate([stack(kkt, True), stack(rt, True)], axis=0))
    rhs = _bf(jnp.concatenate([stack(bt, False), stack(kt, False)], axis=0))
    abig = _dot_nt(lhs, rhs)
    yield
    incl, strict = _chunk_masks(ns, CHUNK)
    a_ab = jnp.where(strict, abig[:ns, :ns], 0.0)
    a_ak = jnp.where(strict, abig[:ns, ns:], 0.0)
    a_rb = jnp.where(incl, abig[ns:, :ns], 0.0)
    a_rk = jnp.where(incl, abig[ns:, ns:], 0.0)

    v_st = _bf(stack(v, True))
    av = _dot(_bf(a_ak), v_st)
    ark = _dot(_bf(a_rk), v_st)
    tmat = yield from _unit_lower_inverse(a_ab, CHUNK)
    sol = _dot(_bf(tmat), _bf(jnp.concatenate([stack(kk0, True), av], axis=1)))
    yield
    ar = _dot(_bf(a_rb), _bf(sol))
    yield
    rm_st = stack(r0, True) - ar[:, :LANES]
    yv_st = ark - ar[:, LANES:]

    ri = _iota((LANES, LANES), 0)
    cj = _iota((LANES, LANES), 1)
    same_head = (ri // RWKV_HEAD) == (cj // RWKV_HEAD)

    ys = []
    for ci in range(n_chunks):
        rows = slice(ci * CHUNK, (ci + 1) * CHUNK)
        wm_c = unstack(sol[:, :LANES], ci)
        uv_c = unstack(sol[:, LANES:], ci)
        s_old = s_ref[...]
        s_bf = _bf(s_old)
        ys.append(_dot_nt(_bf(unstack(rm_st, ci)), s_bf) + unstack(yv_st, ci))
        m_t = jnp.where(same_head, _dot_tn(_bf(wm_c), _bf(bh[rows])), 0.0)
        hv_t = jnp.where(
            same_head,
            _dot_tn(_bf(jnp.concatenate([v[rows], -uv_c], axis=0)),
                    _bf(jnp.concatenate([kh[rows], bh[rows]], axis=0))),
            0.0)
        yield
        s_ref[...] = s_old * dc[ci * CHUNK:ci * CHUNK + 1, :] - _dot(s_bf, _bf(m_t)) + hv_t
        yield
    y = jnp.concatenate(ys, axis=0)

    inv_n = 1.0 / RWKV_HEAD
    mean = _head_half_sum(y, first_half) * inv_n
    yc = y - mean
    var = _head_half_sum(yc * yc, first_half) * inv_n
    yn = yc * lax.rsqrt(var + RWKV_LN_EPS) * lnw + lnb
    bonus = _head_half_sum(r * k2 * rkw, first_half) * v
    return _bf((yn + bonus) * g)


def _rwkv_mix(p, col0, params):
    bsz, t, _ = p.shape
    dim = params["w0"].shape[1]
    wb = RWKV_PAIRS * LANES
    nb = dim // wb
    ct = RWKV_ROWS
    assert t % ct == 0 and col0 % wb == 0 and (col0 + 3 * dim) % 256 == 0 and dim % wb == 0
    cb0 = col0 // wb
    lb0 = (col0 + 3 * dim) // 256

    def col_spec(off):
        return pl.BlockSpec((1, ct, wb), lambda b, h, i: (b, i, cb0 + off + h))

    def row_spec(off):
        return pl.BlockSpec((1, wb), lambda b, h, i: (0, off + h))

    mu = params["mu"]
    in_specs = [
        col_spec(0), col_spec(nb), col_spec(2 * nb),
        pl.BlockSpec((1, ct, 256), lambda b, h, i: (b, i, lb0)),
        row_spec(0), row_spec(nb), row_spec(2 * nb),
        pl.BlockSpec((1, 256), lambda b, h, i: (0, 3 * dim // 256)),
        row_spec(0), row_spec(0), row_spec(0), row_spec(0), row_spec(0), row_spec(0), row_spec(0),
        pl.BlockSpec((params["w2"].shape[0], wb), lambda b, h, i: (0, h)),
        pl.BlockSpec((params["a2"].shape[0], wb), lambda b, h, i: (0, h)),
        pl.BlockSpec((params["g2"].shape[0], wb), lambda b, h, i: (0, h)),
    ]
    return pl.pallas_call(
        _rwkv_kernel,
        grid=(bsz, nb, t // ct),
        in_specs=in_specs,
        out_specs=pl.BlockSpec((1, ct, wb), lambda b, h, i: (b, i, h)),
        out_shape=jax.ShapeDtypeStruct((bsz, t, dim), BF16),
        scratch_shapes=[pltpu.VMEM((ct + SUBLANES, 3 * wb + 256), F32),
                        pltpu.VMEM((RWKV_PAIRS, LANES, LANES), F32)],
        compiler_params=_cparams(("parallel", "parallel", "arbitrary")),
        name="rwkv7_mix",
    )(p, p, p, p, mu, mu, mu, mu, params["w0"], params["a0"], params["k_k"], params["k_a"], params["r_k"],
      params["ln_w"], params["ln_b"], params["w2"], params["a2"], params["g2"])


GDN_ROWS = 4 * CHUNK
GDN_HEADS_PER_STEP = 4


def _causal_conv(sh_ref, lo, hi, w, n_rows):
    ext = sh_ref[pl.ds(0, n_rows + SUBLANES), lo:hi]
    acc = ext[SUBLANES:] * w[CONV_K - 1:CONV_K, :]
    for j in range(CONV_K - 1):
        acc = acc + pltpu.roll(ext, CONV_K - 1 - j, axis=0)[SUBLANES:] * w[j:j + 1, :]
    return acc


def _gdn_kernel(q_ref, k_ref, v_ref, z_ref, gt_ref, cwq_ref, cwk_ref, cwv_ref, alog_ref, dtb_ref, nw_ref,
                o_ref, sh_ref, s_ref):
    ct = GDN_ROWS
    wb = GDN_HEADS_PER_STEP * LANES
    t_idx = pl.program_id(2)

    @pl.when(t_idx == 0)
    def _():
        sh_ref[pl.ds(0, SUBLANES), :] = jnp.zeros((SUBLANES, sh_ref.shape[1]), F32)
        s_ref[...] = jnp.zeros_like(s_ref)

    sh_ref[pl.ds(SUBLANES, ct), 0:wb] = q_ref[0]
    sh_ref[pl.ds(SUBLANES, ct), wb:2 * wb] = k_ref[0]
    sh_ref[pl.ds(SUBLANES, ct), 2 * wb:3 * wb] = v_ref[0]
    q = _silu(_causal_conv(sh_ref, 0, wb, cwq_ref[...], ct))
    k = _silu(_causal_conv(sh_ref, wb, 2 * wb, cwk_ref[...], ct))
    v = _silu(_causal_conv(sh_ref, 2 * wb, 3 * wb, cwv_ref[...], ct))
    sh_ref[pl.ds(0, SUBLANES), :] = sh_ref[pl.ds(ct, SUBLANES), :]

    gates = gt_ref[0]
    beta_all = jax.nn.sigmoid(gates)
    g_all = -jnp.exp(alog_ref[...]) * _softplus(gates + dtb_ref[...])
    z = z_ref[0]
    gens = []
    for p in range(GDN_HEADS_PER_STEP):
        cols = slice(p * LANES, (p + 1) * LANES)
        h_idx = pl.program_id(1) * GDN_HEADS_PER_STEP + p
        gens.append(_gdn_head(q[:, cols], k[:, cols], v[:, cols], z[:, cols], beta_all, g_all, h_idx,
                              nw_ref[...], s_ref.at[p]))
    for p, out in enumerate(_interleave(gens)):
        o_ref[0, :, p * LANES:(p + 1) * LANES] = out


def _gdn_head(q, k, v, z, beta_all, g_all, h_idx, norm_w, s_ref):
    ct = GDN_ROWS
    n_chunks = ct // CHUNK
    q = q * lax.rsqrt(jnp.sum(q * q, axis=-1, keepdims=True) + 1e-6) * (GDN_HEAD ** -0.5)
    k = k * lax.rsqrt(jnp.sum(k * k, axis=-1, keepdims=True) + 1e-6)
    lane = _iota((ct, LANES), 1)
    beta = jnp.sum(jnp.where(lane == h_idx, beta_all, 0.0), axis=-1, keepdims=True)
    g = jnp.sum(jnp.where(lane == h_idx + GDN_HEADS, g_all, 0.0), axis=-1, keepdims=True)

    incl, strict = _chunk_masks(ct, CHUNK)
    gc = _dot_sel_l(_bf(incl.astype(F32)), jnp.broadcast_to(g, (ct, LANES)))
    kb = k * beta
    kq = _dot_nt(_bf(jnp.concatenate([kb, q], axis=0)), _bf(k))
    yield
    gcw = jnp.concatenate([gc] * (ct // LANES), axis=1)
    decay = jnp.where(incl, jnp.exp(jnp.minimum(gcw - gcw.T, 0.0)), 0.0)
    a_mat = jnp.where(strict, kq[:ct] * decay, 0.0)
    qk = kq[ct:] * decay
    tmat = yield from _unit_lower_inverse(a_mat, CHUNK)

    egc = jnp.exp(gc)
    sol = _dot(_bf(tmat), _bf(jnp.concatenate([v * beta, kb * egc], axis=1)))
    yield
    qu = _dot(_bf(qk), _bf(sol))
    yield
    o_v = qu[:, :LANES]
    o_m = q * egc - qu[:, LANES:]
    gl = _rows_bcast(gc, CHUNK - 1, CHUNK)
    kd = k * jnp.exp(gl - gc)
    egl = jnp.exp(gl)

    outs = []
    for ci in range(n_chunks):
        rows = slice(ci * CHUNK, (ci + 1) * CHUNK)
        s_old = s_ref[...]
        s_bf = _bf(s_old)
        outs.append(_dot(_bf(o_m[rows]), s_bf) + o_v[rows])
        hm = _dot_tn(_bf(kd[rows]), _bf(sol[rows]))
        yield
        s_ref[...] = s_old * egl[ci * CHUNK:ci * CHUNK + 1, :] - _dot(_bf(hm[:, LANES:]), s_bf) + hm[:, :LANES]
        yield
    o = jnp.concatenate(outs, axis=0)

    ms = jnp.mean(o * o, axis=-1, keepdims=True)
    return _bf((o * lax.rsqrt(ms + EPS)) * norm_w * _silu(z))


def _gdn_mix(p, col0, gate_col, params):
    bsz, t, _ = p.shape
    ct = GDN_ROWS
    wb = GDN_HEADS_PER_STEP * LANES
    dim = GDN_HEADS * GDN_HEAD
    nb = dim // wb
    assert t % ct == 0 and col0 % wb == 0 and dim % wb == 0 and gate_col % LANES == 0
    cb0 = col0 // wb
    gate_blk = gate_col // LANES

    def col_spec(off):
        return pl.BlockSpec((1, ct, wb), lambda b, h, i: (b, i, cb0 + off + h))

    def cw_spec(off):
        return pl.BlockSpec((CONV_K, wb), lambda b, h, i: (0, off + h))

    full_row = pl.BlockSpec((1, LANES), lambda b, h, i: (0, 0))
    cw = params["conv_w"]
    return pl.pallas_call(
        _gdn_kernel,
        grid=(bsz, nb, t // ct),
        in_specs=[
            col_spec(0), col_spec(nb), col_spec(2 * nb), col_spec(3 * nb),
            pl.BlockSpec((1, ct, LANES), lambda b, h, i: (b, i, gate_blk)),
            cw_spec(0), cw_spec(nb), cw_spec(2 * nb),
            full_row, full_row, full_row,
        ],
        out_specs=pl.BlockSpec((1, ct, wb), lambda b, h, i: (b, i, h)),
        out_shape=jax.ShapeDtypeStruct((bsz, t, dim), BF16),
        scratch_shapes=[pltpu.VMEM((ct + SUBLANES, 3 * wb), F32),
                        pltpu.VMEM((GDN_HEADS_PER_STEP, GDN_HEAD, GDN_HEAD), F32)],
        compiler_params=_cparams(("parallel", "parallel", "arbitrary")),
        name="gated_deltanet_mix",
    )(p, p, p, p, p, cw, cw, cw, params["a_log_row"], params["dt_bias_row"], params["norm_w"])


SSD_CHUNK = 128
SSD_GROUP_CH = 512
SSD_GROUPS_PER_STEP = 4
LOG2E = 1.4426950408889634


def _ssd_kernel(z_ref, x_ref, b_ref, c_ref, dt_ref, cwx_ref, cwb_ref, cwc_ref, cbx_ref, cbb_ref, cbc_ref,
                dtb_ref, alogs_ref, dsk_ref, nw_ref, o_ref, sh_ref, at_ref, st_ref):
    ct = SSD_CHUNK
    gch = SSD_GROUP_CH
    ng = SSD_GROUPS_PER_STEP
    hpg = gch // SSM_HEAD
    xw = ng * gch
    t_idx = pl.program_id(2)

    @pl.when(t_idx == 0)
    def _():
        sh_ref[pl.ds(0, SUBLANES), :] = jnp.zeros((SUBLANES, sh_ref.shape[1]), F32)
        st_ref[...] = jnp.zeros_like(st_ref)

    sh_ref[pl.ds(SUBLANES, ct), 0:xw] = x_ref[0]
    sh_ref[pl.ds(SUBLANES, ct), xw:xw + ng * LANES] = b_ref[0]
    sh_ref[pl.ds(SUBLANES, ct), xw + ng * LANES:xw + 2 * ng * LANES] = c_ref[0]
    xs = _silu(_causal_conv(sh_ref, 0, xw, cwx_ref[...], ct) + cbx_ref[...])
    bm = _silu(_causal_conv(sh_ref, xw, xw + ng * LANES, cwb_ref[...], ct) + cbb_ref[...])
    cm = _silu(_causal_conv(sh_ref, xw + ng * LANES, xw + 2 * ng * LANES, cwc_ref[...], ct) + cbc_ref[...])
    sh_ref[pl.ds(0, SUBLANES), :] = sh_ref[pl.ds(ct, SUBLANES), :]

    dtv = _softplus(dt_ref[0] + dtb_ref[...])
    ri = _iota((ct, ct), 0)
    ci = _iota((ct, ct), 1)
    a_small = dtv * (-jnp.exp(alogs_ref[...]) * LOG2E)
    acum_small = _dot_sel_l(_bf((ri >= ci).astype(F32)), a_small)
    at_ref[...] = acum_small.T
    z = z_ref[0]
    gens = []
    for i in range(ng):
        g_idx = pl.program_id(1) * ng + i
        wide = slice(i * gch, (i + 1) * gch)
        nar = slice(i * LANES, (i + 1) * LANES)
        acum_rows = at_ref[pl.ds(pl.multiple_of(g_idx * hpg, hpg), hpg), :]
        gens.append(_ssd_group(g_idx, xs[:, wide], bm[:, nar], cm[:, nar], z[:, wide], dtv, acum_small, acum_rows,
                               dsk_ref[:, wide], nw_ref[:, wide], st_ref.at[i]))
    for i, out in enumerate(_interleave(gens)):
        o_ref[0, :, i * gch:(i + 1) * gch] = out


def _ssd_group(g_idx, xs, bm, cm, z, dtv, acum_small, acum_rows, dskip, norm_w, st_ref):
    ct = SSD_CHUNK
    gch = SSD_GROUP_CH
    hpg = gch // SSM_HEAD
    expand = _bf((_iota((LANES, gch), 0) == g_idx * hpg + _iota((LANES, gch), 1) // SSM_HEAD).astype(F32))
    both = _dot_sel_r(jnp.concatenate([dtv, acum_small], axis=0), expand)
    yield
    dt_exp = both[:ct]
    acum = both[ct:]
    xdt = xs * dt_exp

    incl = _iota((ct, ct), 0) >= _iota((ct, ct), 1)
    cb = jnp.where(incl, _dot_nt(_bf(cm), _bf(bm)), 0.0)
    st_old = st_ref[...]
    y_state = _dot(_bf(cm), _bf(st_old)) * jnp.exp2(acum)
    yield

    first_half = _iota((ct, LANES), 1) < SSM_HEAD
    y_parts = []
    for jp in range(hpg // 2):
        x_pair = _bf(xdt[:, jp * LANES:(jp + 1) * LANES])
        pair = []
        for j in (2 * jp, 2 * jp + 1):
            col = jnp.broadcast_to(acum[:, j * SSM_HEAD:j * SSM_HEAD + 1], (ct, ct))
            row = jnp.broadcast_to(acum_rows[j:j + 1, :], (ct, ct))
            pair.append(_dot(_bf(cb * jnp.exp2(jnp.minimum(col - row, 0.0))), x_pair))
        y_parts.append(jnp.where(first_half, pair[0], pair[1]))
        yield
    y = jnp.concatenate(y_parts, axis=1) + y_state + xs * dskip

    a_last = acum[ct - 1:ct, :]
    st_ref[...] = st_old * jnp.exp2(a_last) + _dot_tn(_bf(bm), _bf(xdt * jnp.exp2(a_last - acum)))
    yield

    yg = y * _silu(z)
    ms = jnp.mean(yg * yg, axis=-1, keepdims=True)
    return _bf((yg * lax.rsqrt(ms + EPS)) * norm_w)


def _ssd_mix(zx, params):
    bsz, t, _ = zx.shape
    ct = SSD_CHUNK
    gps = SSD_GROUPS_PER_STEP
    xw = gps * SSD_GROUP_CH
    nw = gps * LANES
    d_inner = params["norm_w"].shape[1]
    nsteps = d_inner // xw
    assert t % ct == 0 and d_inner == SSM_GROUPS * SSD_GROUP_CH and d_inner % xw == 0
    xb = d_inner // xw
    bb = 2 * d_inner // nw
    cbk = bb + nsteps
    dtb = (2 * d_inner + 2 * SSM_GROUPS * SSM_STATE) // LANES

    cw = params["conv_w"]
    cbias = params["conv_b"]
    wide = lambda off: pl.BlockSpec((1, ct, xw), lambda b, g, i: (b, i, off + g))
    narrow = lambda off: pl.BlockSpec((1, ct, nw), lambda b, g, i: (b, i, off + g))
    row_wide = pl.BlockSpec((1, xw), lambda b, g, i: (0, g))
    row_full = pl.BlockSpec((1, LANES), lambda b, g, i: (0, 0))
    cb_b = d_inner // nw
    return pl.pallas_call(
        _ssd_kernel,
        grid=(bsz, nsteps, t // ct),
        in_specs=[
            wide(0), wide(xb), narrow(bb), narrow(cbk),
            pl.BlockSpec((1, ct, LANES), lambda b, g, i: (b, i, dtb)),
            pl.BlockSpec((CONV_K, xw), lambda b, g, i: (0, g)),
            pl.BlockSpec((CONV_K, nw), lambda b, g, i: (0, cb_b + g)),
            pl.BlockSpec((CONV_K, nw), lambda b, g, i: (0, cb_b + nsteps + g)),
            pl.BlockSpec((1, xw), lambda b, g, i: (0, g)),
            pl.BlockSpec((1, nw), lambda b, g, i: (0, cb_b + g)),
            pl.BlockSpec((1, nw), lambda b, g, i: (0, cb_b + nsteps + g)),
            row_full, row_full, row_wide, row_wide,
        ],
        out_specs=pl.BlockSpec((1, ct, xw), lambda b, g, i: (b, i, g)),
        out_shape=jax.ShapeDtypeStruct((bsz, t, d_inner), BF16),
        scratch_shapes=[
            pltpu.VMEM((ct + SUBLANES, xw + 2 * nw), F32),
            pltpu.VMEM((LANES, ct), F32),
            pltpu.VMEM((gps, SSM_STATE, SSD_GROUP_CH), F32),
        ],
        compiler_params=_cparams(("parallel", "parallel", "arbitrary")),
        name="mamba2_ssd_mix",
    )(zx, zx, zx, zx, zx, cw, cw, cw, cbias, cbias, cbias, params["dt_bias_row"], params["a_log_row"],
      params["d_exp"], params["norm_w"])


def _pad_cols(w, n):
    return jnp.pad(w, ((0, 0), (0, n - w.shape[1])))


def _lane_row(v, offset):
    return jnp.pad(v, (offset, LANES - offset - v.shape[0])).reshape(1, LANES)


def _mods(mod, bsz, d):
    mod = mod[:bsz]
    return mod[:, None, 0:d], mod[:, None, d:2 * d], mod[:, None, 2 * d:3 * d]


def kernel(x, c, ada_mix_w, ada_mix_b, ada_ffn_w, ada_ffn_b, hg_w_in, hg_w_out, rwkv_mu, rwkv_w0, rwkv_w2,
           rwkv_a0, rwkv_a2, rwkv_g2, rwkv_k_k, rwkv_k_a, rwkv_r_k, rwkv_ln_w, rwkv_ln_b, gdn_conv_w,
           gdn_a_log, gdn_dt_bias, gdn_norm_w, ssm_w_in, ssm_conv_w, ssm_conv_b, ssm_dt_bias, ssm_a_log,
           ssm_d, ssm_norm_w, ssm_w_out, ffn_w1, ffn_w3, ffn_w2, final_norm_w):
    bsz, t, d = x.shape
    depth = ada_mix_w.shape[0]
    assert bsz <= SUBLANES

    c_pad = jnp.pad(c, ((0, SUBLANES - bsz), (0, 0)))
    mix_mod = _adaln(c_pad, ada_mix_w, ada_mix_b)
    ffn_mod = _adaln(c_pad, ada_ffn_w, ada_ffn_b)
    final_w = final_norm_w.reshape(1, d)
    w1_bf, w3_bf, w2_bf = _bf(ffn_w1), _bf(ffn_w3), _bf(ffn_w2)

    for i in range(depth):
        j = i // 2
        shift, scale, gate = _mods(mix_mod[i], bsz, d)
        if i % 2 == 0:
            rwkv_dim = rwkv_w0.shape[1]
            gdn_dim = GDN_HEADS * GDN_HEAD
            rwkv_cols = 3 * rwkv_dim + rwkv_w2.shape[1] + rwkv_a2.shape[1] + rwkv_g2.shape[1]
            w_in = _bf(hg_w_in[j])
            w_in = jnp.concatenate(
                [w_in[:, rwkv_cols:rwkv_cols + 4 * gdn_dim], w_in[:, :rwkv_cols], w_in[:, rwkv_cols + 4 * gdn_dim:]],
                axis=1)
            p = _norm_proj(x, shift, scale, _pad_cols(w_in, 7680), tm=1024, tn=1280)
            row = lambda v: v.reshape(1, -1)
            y_a = _rwkv_mix(p, 4 * gdn_dim, dict(
                mu=row(rwkv_mu[j]), w0=row(rwkv_w0[j]), a0=row(rwkv_a0[j]), k_k=row(rwkv_k_k[j]),
                k_a=row(rwkv_k_a[j]), r_k=row(rwkv_r_k[j]), ln_w=row(rwkv_ln_w[j]), ln_b=row(rwkv_ln_b[j]),
                w2=_bf(rwkv_w2[j]), a2=_bf(rwkv_a2[j]), g2=_bf(rwkv_g2[j])))
            y_b = _gdn_mix(p, 0, 4 * gdn_dim + rwkv_cols, dict(
                conv_w=gdn_conv_w[j], a_log_row=_lane_row(gdn_a_log[j], GDN_HEADS),
                dt_bias_row=_lane_row(gdn_dt_bias[j], GDN_HEADS), norm_w=row(gdn_norm_w[j])))
            w_out = _bf(hg_w_out[j])
            x = _proj_residual([y_a, y_b], [w_out[:rwkv_dim], w_out[rwkv_dim:rwkv_dim + gdn_dim]], x, gate,
                               tm=1024, tn=512)
        else:
            d_inner = ssm_norm_w.shape[1]
            zx = _norm_proj(x, shift, scale, _pad_cols(_bf(ssm_w_in[j]), 10752), tm=1024, tn=1536)
            y = _ssd_mix(zx, dict(
                conv_w=ssm_conv_w[j], conv_b=ssm_conv_b[j].reshape(1, -1),
                dt_bias_row=_lane_row(ssm_dt_bias[j], 0), a_log_row=_lane_row(ssm_a_log[j], 0),
                d_exp=jnp.repeat(ssm_d[j], SSM_HEAD).reshape(1, -1),
                norm_w=ssm_norm_w[j].reshape(1, d_inner)))
            x = _proj_residual([y], [_bf(ssm_w_out[j])], x, gate, tm=1024, tn=512)
        shift, scale, gate = _mods(ffn_mod[i], bsz, d)
        x = _ffn(x, shift, scale, gate, w1_bf, w3_bf, w2_bf, i, final_w, final_norm=(i == depth - 1), tm=512, tf=512)
    return x
```

```python
import functools

import jax
import jax.numpy as jnp
from jax import lax
from jax.experimental import pallas as pl
from jax.experimental.pallas import tpu as pltpu

F32 = jnp.float32
BF16 = jnp.bfloat16

EPS = 1e-5
RWKV_HEAD = 64
RWKV_LN_EPS = 64e-5
GDN_HEAD = 128
GDN_HEADS = 8
SSM_HEAD = 64
SSM_GROUPS = 8
SSM_STATE = 128
CONV_K = 4

LANES = 128
SUBLANES = 8
CHUNK = 64
VMEM_LIMIT = 56 * 1024 * 1024


def _cparams(sem):
    return pltpu.CompilerParams(dimension_semantics=sem, vmem_limit_bytes=VMEM_LIMIT)


def _bf(x):
    return x.astype(BF16)


def _dot(a, b):
    return jnp.dot(a, b, preferred_element_type=F32)


def _dot_nt(a, b):
    return lax.dot_general(a, b, (((1,), (1,)), ((), ())), preferred_element_type=F32)


def _dot_tn(a, b):
    return lax.dot_general(a, b, (((0,), (0,)), ((), ())), preferred_element_type=F32)


def _split2(x):
    hi = _bf(x)
    lo = _bf(x - hi.astype(F32))
    return hi, lo


def _split3(x):
    hi = _bf(x)
    r = x - hi.astype(F32)
    mid = _bf(r)
    lo = _bf(r - mid.astype(F32))
    return hi, mid, lo


def _dot3(a, b):
    ah, al = _split2(a)
    bh, bl = _split2(b)
    return _dot(ah, bh) + (_dot(ah, bl) + _dot(al, bh))


def _dot_sel_l(sel, x):
    hi, mid, lo = _split3(x)
    return _dot(sel, hi) + (_dot(sel, mid) + _dot(sel, lo))


def _dot_sel_r(x, sel):
    hi, mid, lo = _split3(x)
    return _dot(hi, sel) + (_dot(mid, sel) + _dot(lo, sel))


def _silu(x):
    h = 0.5 * x
    return h + h * jnp.tanh(h)


def _softplus(x):
    return jnp.maximum(x, 0.0) + jnp.log(1.0 + jnp.exp(-jnp.abs(x)))


def _iota(shape, dim):
    return lax.broadcasted_iota(jnp.int32, shape, dim)


def _chunk_masks(n, chunk):
    ri = _iota((n, n), 0)
    ci = _iota((n, n), 1)
    same = (ri // chunk) == (ci // chunk)
    return same & (ri >= ci), same & (ri > ci)


INV_BASE = 8


def _unit_lower_inverse(a_strict, chunk):
    n = a_strict.shape[0]
    ri = _iota((n, n), 0)
    ci = _iota((n, n), 1)
    q = jnp.where((ri // INV_BASE) == (ci // INV_BASE), -a_strict, 0.0)
    t = (ri == ci).astype(F32) + q
    for _ in range(2):
        qb = _bf(q)
        q = _dot(qb, qb)
        yield
        t = t + _dot(_bf(t), _bf(q))
        yield
    s = INV_BASE
    while s < chunk:
        off = ((ri // (2 * s)) == (ci // (2 * s))) & ((ri // s) != (ci // s))
        tb = _bf(t)
        x = _dot(_bf(jnp.where(off, a_strict, 0.0)), tb)
        yield
        t = t - _dot(tb, _bf(x))
        yield
        s *= 2
    return t


def _lockstep(gens):
    results = [None] * len(gens)
    live = list(enumerate(gens))
    while live:
        still = []
        for i, gen in live:
            try:
                next(gen)
                still.append((i, gen))
            except StopIteration as done:
                results[i] = done.value
        live = still
        if live:
            yield
    return results


def _interleave(gens):
    runner = _lockstep(gens)
    while True:
        try:
            next(runner)
        except StopIteration as done:
            return done.value


def _rows_bcast(x, rows, chunk):
    n = x.shape[0]
    parts = []
    for c0 in range(0, n, chunk):
        parts.append(jnp.broadcast_to(x[c0 + rows:c0 + rows + 1, :], (chunk, x.shape[1])))
    return jnp.concatenate(parts, axis=0) if len(parts) > 1 else parts[0]


def _adaln_kernel(c_ref, w_ref, b_ref, o_ref):
    o_ref[0] = _dot3(_silu(c_ref[...]), w_ref[0]) + b_ref[0]


def _adaln(c_pad, w, b):
    depth, d, n = w.shape
    tn = 1536
    assert n % tn == 0
    return pl.pallas_call(
        _adaln_kernel,
        grid=(depth, n // tn),
        in_specs=[
            pl.BlockSpec((SUBLANES, d), lambda l, j: (0, 0)),
            pl.BlockSpec((1, d, tn), lambda l, j: (l, 0, j)),
            pl.BlockSpec((1, 1, tn), lambda l, j: (l, 0, j)),
        ],
        out_specs=pl.BlockSpec((1, SUBLANES, tn), lambda l, j: (l, 0, j)),
        out_shape=jax.ShapeDtypeStruct((depth, SUBLANES, n), F32),
        compiler_params=_cparams(("parallel", "parallel")),
        name="adaln_mod",
    )(c_pad, w, b.reshape(depth, 1, n))


def _modulated_norm(x, shift, scale):
    ms = jnp.mean(x * x, axis=-1, keepdims=True)
    return (x * lax.rsqrt(ms + EPS)) * (1.0 + scale) + shift


def _norm_proj_kernel(x_ref, shift_ref, scale_ref, w_ref, o_ref, h_ref):
    @pl.when(pl.program_id(2) == 0)
    def _():
        h_ref[...] = _bf(_modulated_norm(x_ref[0], shift_ref[0], scale_ref[0]))

    o_ref[0] = _dot(h_ref[...], w_ref[...])


def _norm_proj(x, shift, scale, w_bf16, tm, tn):
    bsz, t, d = x.shape
    n = w_bf16.shape[1]
    assert t % tm == 0 and n % tn == 0
    return pl.pallas_call(
        _norm_proj_kernel,
        grid=(bsz, t // tm, n // tn),
        in_specs=[
            pl.BlockSpec((1, tm, d), lambda b, i, j: (b, i, 0)),
            pl.BlockSpec((1, 1, d), lambda b, i, j: (b, 0, 0)),
            pl.BlockSpec((1, 1, d), lambda b, i, j: (b, 0, 0)),
            pl.BlockSpec((d, tn), lambda b, i, j: (0, j)),
        ],
        out_specs=pl.BlockSpec((1, tm, tn), lambda b, i, j: (b, i, j)),
        out_shape=jax.ShapeDtypeStruct((bsz, t, n), F32),
        scratch_shapes=[pltpu.VMEM((tm, d), BF16)],
        compiler_params=_cparams(("parallel", "parallel", "arbitrary")),
        name="norm_proj",
    )(x, shift, scale, w_bf16)


def _proj_residual_kernel(n_in, *refs):
    y_refs = refs[:n_in]
    w_refs = refs[n_in:2 * n_in]
    x_ref, gate_ref, o_ref = refs[2 * n_in:]
    acc = _dot(y_refs[0][0], w_refs[0][...])
    for y_ref, w_ref in zip(y_refs[1:], w_refs[1:]):
        acc = acc + _dot(y_ref[0], w_ref[...])
    o_ref[0] = x_ref[0] + gate_ref[0] * acc


def _proj_residual(ys, ws, x, gate, tm, tn):
    bsz, t, d = x.shape
    n_in = len(ys)
    assert t % tm == 0 and d % tn == 0
    in_specs = [pl.BlockSpec((1, tm, y.shape[2]), lambda b, i, j: (b, i, 0)) for y in ys]
    in_specs += [pl.BlockSpec((w.shape[0], tn), lambda b, i, j: (0, j)) for w in ws]
    in_specs += [
        pl.BlockSpec((1, tm, tn), lambda b, i, j: (b, i, j)),
        pl.BlockSpec((1, 1, tn), lambda b, i, j: (b, 0, j)),
    ]
    return pl.pallas_call(
        functools.partial(_proj_residual_kernel, n_in),
        grid=(bsz, t // tm, d // tn),
        in_specs=in_specs,
        out_specs=pl.BlockSpec((1, tm, tn), lambda b, i, j: (b, i, j)),
        out_shape=jax.ShapeDtypeStruct((bsz, t, d), F32),
        compiler_params=_cparams(("parallel", "parallel", "arbitrary")),
        name="proj_residual",
    )(*ys, *ws, x, gate)


def _ffn_kernel(final_norm, x_ref, shift_ref, scale_ref, gate_ref, w1_ref, w3_ref, w2_ref, fw_ref, o_ref,
                h_ref):
    f = pl.program_id(2)

    @pl.when(f == 0)
    def _():
        h_ref[...] = _bf(_modulated_norm(x_ref[0], shift_ref[0], scale_ref[0]))
        o_ref[...] = jnp.zeros_like(o_ref)

    h = h_ref[...]
    a = _dot(h, w1_ref[...])
    b = _dot(h, w3_ref[...])
    o_ref[0] += _dot(_bf(_silu(a) * b), w2_ref[...])

    @pl.when(f == pl.num_programs(2) - 1)
    def _():
        y = x_ref[0] + gate_ref[0] * o_ref[0]
        if final_norm:
            ms = jnp.mean(y * y, axis=-1, keepdims=True)
            y = (y * lax.rsqrt(ms + EPS)) * fw_ref[...]
        o_ref[0] = y


def _ffn(x, shift, scale, gate, w1, w3, w2, layer, final_w, final_norm, tm, tf):
    bsz, t, d = x.shape
    hidden = w1.shape[2]
    assert t % tm == 0 and hidden % tf == 0
    return pl.pallas_call(
        functools.partial(_ffn_kernel, final_norm),
        grid=(bsz, t // tm, hidden // tf),
        in_specs=[
            pl.BlockSpec((1, tm, d), lambda b, i, f: (b, i, 0)),
            pl.BlockSpec((1, 1, d), lambda b, i, f: (b, 0, 0)),
            pl.BlockSpec((1, 1, d), lambda b, i, f: (b, 0, 0)),
            pl.BlockSpec((1, 1, d), lambda b, i, f: (b, 0, 0)),
            pl.BlockSpec((None, d, tf), lambda b, i, f: (layer, 0, f)),
            pl.BlockSpec((None, d, tf), lambda b, i, f: (layer, 0, f)),
            pl.BlockSpec((None, tf, d), lambda b, i, f: (layer, f, 0)),
            pl.BlockSpec((1, d), lambda b, i, f: (0, 0)),
        ],
        out_specs=pl.BlockSpec((1, tm, d), lambda b, i, f: (b, i, 0)),
        out_shape=jax.ShapeDtypeStruct((bsz, t, d), F32),
        scratch_shapes=[pltpu.VMEM((tm, d), BF16)],
        compiler_params=_cparams(("parallel", "parallel", "arbitrary")),
        name="swiglu_ffn",
    )(x, shift, scale, gate, w1, w3, w2, final_w)


RWKV_ROWS = 4 * CHUNK
RWKV_PAIRS = 8


def _head_half_sum(x, first_half):
    s0 = jnp.sum(jnp.where(first_half, x, 0.0), axis=-1, keepdims=True)
    s1 = jnp.sum(jnp.where(first_half, 0.0, x), axis=-1, keepdims=True)
    return jnp.where(first_half, s0, s1)


def _rwkv_kernel(pr_ref, pk_ref, pv_ref, pl_ref, mur_ref, muk_ref, muv_ref, mul_ref, w0_ref, a0_ref, kkw_ref,
                 kaw_ref, rkw_ref, lnw_ref, lnb_ref, w2_ref, a2_ref, g2_ref, o_ref, sh_ref, s_ref):
    ct = RWKV_ROWS
    wb = RWKV_PAIRS * LANES
    t_idx = pl.program_id(2)

    @pl.when(t_idx == 0)
    def _():
        sh_ref[pl.ds(0, SUBLANES), :] = jnp.zeros((SUBLANES, sh_ref.shape[1]), F32)
        s_ref[...] = jnp.zeros_like(s_ref)

    sh_ref[pl.ds(SUBLANES, ct), 0:wb] = pr_ref[0]
    sh_ref[pl.ds(SUBLANES, ct), wb:2 * wb] = pk_ref[0]
    sh_ref[pl.ds(SUBLANES, ct), 2 * wb:3 * wb] = pv_ref[0]
    sh_ref[pl.ds(SUBLANES, ct), 3 * wb:3 * wb + 256] = pl_ref[0]

    def lerp(lo, hi, mu_ref):
        cur = sh_ref[pl.ds(SUBLANES, ct), lo:hi]
        prev = sh_ref[pl.ds(SUBLANES - 1, ct), lo:hi]
        return cur + mu_ref[...] * (prev - cur)

    r = lerp(0, wb, mur_ref)
    k = lerp(wb, 2 * wb, muk_ref)
    v = lerp(2 * wb, 3 * wb, muv_ref)
    xl = lerp(3 * wb, 3 * wb + 256, mul_ref)
    sh_ref[pl.ds(0, SUBLANES), :] = sh_ref[pl.ds(ct, SUBLANES), :]
    pw = xl[:, 0:64]
    pa = xl[:, 64:128]
    pg = xl[:, 128:256]

    w = -_softplus(-(w0_ref[...] + _dot(_bf(jnp.tanh(pw)), w2_ref[...]))) - 0.5
    logd = -jnp.exp(w)
    a = jax.nn.sigmoid(a0_ref[...] + _dot(_bf(pa), a2_ref[...]))
    g = _dot(_bf(jax.nn.sigmoid(pg)), g2_ref[...])

    gens = []
    for p in range(RWKV_PAIRS):
        cols = slice(p * LANES, (p + 1) * LANES)
        gens.append(_rwkv_pair(
            r[:, cols], k[:, cols], v[:, cols], logd[:, cols], a[:, cols], g[:, cols], kkw_ref[:, cols],
            kaw_ref[:, cols], rkw_ref[:, cols], lnw_ref[:, cols], lnb_ref[:, cols], s_ref.at[p]))
    for p, out in enumerate(_interleave(gens)):
        o_ref[0, :, p * LANES:(p + 1) * LANES] = out


def _rwkv_chunk_prep(kkt, rt, bt, kt, kk0, r0, v):
    fh = _iota((CHUNK, LANES), 1) < RWKV_HEAD
    ns = 2 * CHUNK

    def stack(x, masked):
        if masked:
            return jnp.concatenate([jnp.where(fh, x, 0.0), jnp.where(fh, 0.0, x)], axis=0)
        return jnp.concatenate([x, x], axis=0)

    def unstack(x_st):
        return x_st[:CHUNK] + x_st[CHUNK:]

    lhs = _bf(jnp.concatenate([stack(kkt, True), stack(rt, True)], axis=0))
    rhs = _bf(jnp.concatenate([stack(bt, False), stack(kt, False)], axis=0))
    abig = _dot_nt(lhs, rhs)
    yield
    incl, strict = _chunk_masks(ns, CHUNK)
    a_ab = jnp.where(strict, abig[:ns, :ns], 0.0)
    a_ak = jnp.where(strict, abig[:ns, ns:], 0.0)
    a_rb = jnp.where(incl, abig[ns:, :ns], 0.0)
    a_rk = jnp.where(incl, abig[ns:, ns:], 0.0)

    v_st = _bf(stack(v, True))
    av = _dot(_bf(a_ak), v_st)
    ark = _dot(_bf(a_rk), v_st)
    tmat = yield from _unit_lower_inverse(a_ab, CHUNK)
    sol = _dot(_bf(tmat), _bf(jnp.concatenate([stack(kk0, True), av], axis=1)))
    yield
    ar = _dot(_bf(a_rb), _bf(sol))
    yield
    return (unstack(sol[:, :LANES]), unstack(sol[:, LANES:]), r0 - unstack(ar[:, :LANES]),
            unstack(ark - ar[:, LANES:]))


def _rwkv_pair(r, k, v, logd, a, g, kkw, kaw, rkw, lnw, lnb, s_ref):
    ct = RWKV_ROWS
    n_chunks = ct // CHUNK
    first_half = _iota((ct, LANES), 1) < RWKV_HEAD
    kk = k * kkw
    kk = kk * lax.rsqrt(_head_half_sum(kk * kk, first_half) + 1e-24)
    k2 = k * (1.0 + (a - 1.0) * kaw)
    b = kk * a

    incl_ct, _ = _chunk_masks(ct, CHUNK)
    c = _dot_sel_l(_bf(incl_ct.astype(F32)), logd)
    yield
    cm1 = c - logd
    cref = _rows_bcast(c, CHUNK // 2 - 1, CHUNK)
    cend = _rows_bcast(c, CHUNK - 1, CHUNK)
    e_neg = jnp.exp(cref - c)
    rt = r * jnp.exp(c - cref)
    kkt = kk * jnp.exp(cm1 - cref)
    bt = b * e_neg
    kt = k2 * e_neg
    kk0 = kk * jnp.exp(cm1)
    r0 = r * jnp.exp(c)
    e_end = jnp.exp(cend - c)
    bh = b * e_end
    kh = k2 * e_end
    dc = jnp.exp(cend)

    preps = yield from _lockstep([
        _rwkv_chunk_prep(*(x[ci * CHUNK:(ci + 1) * CHUNK] for x in (kkt, rt, bt, kt, kk0, r0, v)))
        for ci in range(n_chunks)])

    ri = _iota((LANES, LANES), 0)
    cj = _iota((LANES, LANES), 1)
    same_head = (ri // RWKV_HEAD) == (cj // RWKV_HEAD)

    ys = []
    for ci in range(n_chunks):
        rows = slice(ci * CHUNK, (ci + 1) * CHUNK)
        wm_c, uv_c, rm_c, yv_c = preps[ci]
        s_old = s_ref[...]
        s_bf = _bf(s_old)
        ys.append(_dot_nt(_bf(rm_c), s_bf) + yv_c)
        m_t = jnp.where(same_head, _dot_tn(_bf(wm_c), _bf(bh[rows])), 0.0)
        hv_t = jnp.where(
            same_head,
            _dot_tn(_bf(jnp.concatenate([v[rows], -uv_c], axis=0)),
                    _bf(jnp.concatenate([kh[rows], bh[rows]], axis=0))),
            0.0)
        yield
        s_ref[...] = s_old * dc[ci * CHUNK:ci * CHUNK + 1, :] - _dot(s_bf, _bf(m_t)) + hv_t
        yield
    y = jnp.concatenate(ys, axis=0)

    inv_n = 1.0 / RWKV_HEAD
    mean = _head_half_sum(y, first_half) * inv_n
    yc = y - mean
    var = _head_half_sum(yc * yc, first_half) * inv_n
    yn = yc * lax.rsqrt(var + RWKV_LN_EPS) * lnw + lnb
    bonus = _head_half_sum(r * k2 * rkw, first_half) * v
    return _bf((yn + bonus) * g)


def _rwkv_mix(p, col0, params):
    bsz, t, _ = p.shape
    dim = params["w0"].shape[1]
    wb = RWKV_PAIRS * LANES
    nb = dim // wb
    ct = RWKV_ROWS
    assert t % ct == 0 and col0 % wb == 0 and (col0 + 3 * dim) % 256 == 0 and dim % wb == 0
    cb0 = col0 // wb
    lb0 = (col0 + 3 * dim) // 256

    def col_spec(off):
        return pl.BlockSpec((1, ct, wb), lambda b, h, i: (b, i, cb0 + off + h))

    def row_spec(off):
        return pl.BlockSpec((1, wb), lambda b, h, i: (0, off + h))

    mu = params["mu"]
    in_specs = [
        col_spec(0), col_spec(nb), col_spec(2 * nb),
        pl.BlockSpec((1, ct, 256), lambda b, h, i: (b, i, lb0)),
        row_spec(0), row_spec(nb), row_spec(2 * nb),
        pl.BlockSpec((1, 256), lambda b, h, i: (0, 3 * dim // 256)),
        row_spec(0), row_spec(0), row_spec(0), row_spec(0), row_spec(0), row_spec(0), row_spec(0),
        pl.BlockSpec((params["w2"].shape[0], wb), lambda b, h, i: (0, h)),
        pl.BlockSpec((params["a2"].shape[0], wb), lambda b, h, i: (0, h)),
        pl.BlockSpec((params["g2"].shape[0], wb), lambda b, h, i: (0, h)),
    ]
    return pl.pallas_call(
        _rwkv_kernel,
        grid=(bsz, nb, t // ct),
        in_specs=in_specs,
        out_specs=pl.BlockSpec((1, ct, wb), lambda b, h, i: (b, i, h)),
        out_shape=jax.ShapeDtypeStruct((bsz, t, dim), BF16),
        scratch_shapes=[pltpu.VMEM((ct + SUBLANES, 3 * wb + 256), F32),
                        pltpu.VMEM((RWKV_PAIRS, LANES, LANES), F32)],
        compiler_params=_cparams(("parallel", "parallel", "arbitrary")),
        name="rwkv7_mix",
    )(p, p, p, p, mu, mu, mu, mu, params["w0"], params["a0"], params["k_k"], params["k_a"], params["r_k"],
      params["ln_w"], params["ln_b"], params["w2"], params["a2"], params["g2"])


GDN_ROWS = 4 * CHUNK
GDN_HEADS_PER_STEP = 8


def _causal_conv(sh_ref, lo, hi, w, n_rows):
    ext = sh_ref[pl.ds(0, n_rows + SUBLANES), lo:hi]
    acc = ext[SUBLANES:] * w[CONV_K - 1:CONV_K, :]
    for j in range(CONV_K - 1):
        acc = acc + pltpu.roll(ext, CONV_K - 1 - j, axis=0)[SUBLANES:] * w[j:j + 1, :]
    return acc


def _gdn_kernel(q_ref, k_ref, v_ref, z_ref, gt_ref, cwq_ref, cwk_ref, cwv_ref, alog_ref, dtb_ref, nw_ref,
                o_ref, sh_ref, s_ref):
    ct = GDN_ROWS
    wb = GDN_HEADS_PER_STEP * LANES
    t_idx = pl.program_id(2)

    @pl.when(t_idx == 0)
    def _():
        sh_ref[pl.ds(0, SUBLANES), :] = jnp.zeros((SUBLANES, sh_ref.shape[1]), F32)
        s_ref[...] = jnp.zeros_like(s_ref)

    sh_ref[pl.ds(SUBLANES, ct), 0:wb] = q_ref[0]
    sh_ref[pl.ds(SUBLANES, ct), wb:2 * wb] = k_ref[0]
    sh_ref[pl.ds(SUBLANES, ct), 2 * wb:3 * wb] = v_ref[0]
    q = _silu(_causal_conv(sh_ref, 0, wb, cwq_ref[...], ct))
    k = _silu(_causal_conv(sh_ref, wb, 2 * wb, cwk_ref[...], ct))
    v = _silu(_causal_conv(sh_ref, 2 * wb, 3 * wb, cwv_ref[...], ct))
    sh_ref[pl.ds(0, SUBLANES), :] = sh_ref[pl.ds(ct, SUBLANES), :]

    gates = gt_ref[0]
    beta_all = jax.nn.sigmoid(gates)
    g_all = -jnp.exp(alog_ref[...]) * _softplus(gates + dtb_ref[...])
    z = z_ref[0]
    gens = []
    for p in range(GDN_HEADS_PER_STEP):
        cols = slice(p * LANES, (p + 1) * LANES)
        h_idx = pl.program_id(1) * GDN_HEADS_PER_STEP + p
        gens.append(_gdn_head(q[:, cols], k[:, cols], v[:, cols], z[:, cols], beta_all, g_all, h_idx,
                              nw_ref[...], s_ref.at[p]))
    for p, out in enumerate(_interleave(gens)):
        o_ref[0, :, p * LANES:(p + 1) * LANES] = out


def _gdn_tile_prep(q, k, kb, vb, kbe, gc):
    n = 2 * CHUNK
    incl, strict = _chunk_masks(n, CHUNK)
    decay = jnp.where(incl, jnp.exp(jnp.minimum(gc - gc.T, 0.0)), 0.0)
    kq = _dot_nt(_bf(jnp.concatenate([kb, q], axis=0)), _bf(k))
    yield
    a_mat = jnp.where(strict, kq[:n] * decay, 0.0)
    qk = kq[n:] * decay
    tmat = yield from _unit_lower_inverse(a_mat, CHUNK)
    sol = _dot(_bf(tmat), _bf(jnp.concatenate([vb, kbe], axis=1)))
    yield
    qu = _dot(_bf(qk), _bf(sol))
    yield
    return sol, qu


def _gdn_head(q, k, v, z, beta_all, g_all, h_idx, norm_w, s_ref):
    ct = GDN_ROWS
    n_chunks = ct // CHUNK
    q = q * lax.rsqrt(jnp.sum(q * q, axis=-1, keepdims=True) + 1e-6) * (GDN_HEAD ** -0.5)
    k = k * lax.rsqrt(jnp.sum(k * k, axis=-1, keepdims=True) + 1e-6)
    lane = _iota((ct, LANES), 1)
    beta = jnp.sum(jnp.where(lane == h_idx, beta_all, 0.0), axis=-1, keepdims=True)
    g = jnp.sum(jnp.where(lane == h_idx + GDN_HEADS, g_all, 0.0), axis=-1, keepdims=True)

    incl, _ = _chunk_masks(ct, CHUNK)
    gc = _dot_sel_l(_bf(incl.astype(F32)), jnp.broadcast_to(g, (ct, LANES)))
    yield
    kb = k * beta
    egc = jnp.exp(gc)
    vb = v * beta
    kbe = kb * egc
    tiles = yield from _lockstep([
        _gdn_tile_prep(*(x[t0:t0 + LANES] for x in (q, k, kb, vb, kbe, gc))) for t0 in range(0, ct, LANES)])
    sol = jnp.concatenate([s for s, _ in tiles], axis=0)
    qu = jnp.concatenate([u for _, u in tiles], axis=0)
    o_v = qu[:, :LANES]
    o_m = q * egc - qu[:, LANES:]
    gl = _rows_bcast(gc, CHUNK - 1, CHUNK)
    kd = k * jnp.exp(gl - gc)
    egl = jnp.exp(gl)

    outs = []
    for ci in range(n_chunks):
        rows = slice(ci * CHUNK, (ci + 1) * CHUNK)
        s_old = s_ref[...]
        s_bf = _bf(s_old)
        outs.append(_dot(_bf(o_m[rows]), s_bf) + o_v[rows])
        hm = _dot_tn(_bf(kd[rows]), _bf(sol[rows]))
        yield
        s_ref[...] = s_old * egl[ci * CHUNK:ci * CHUNK + 1, :] - _dot(_bf(hm[:, LANES:]), s_bf) + hm[:, :LANES]
        yield
    o = jnp.concatenate(outs, axis=0)

    ms = jnp.mean(o * o, axis=-1, keepdims=True)
    return _bf((o * lax.rsqrt(ms + EPS)) * norm_w * _silu(z))


def _gdn_mix(p, col0, gate_col, params):
    bsz, t, _ = p.shape
    ct = GDN_ROWS
    wb = GDN_HEADS_PER_STEP * LANES
    dim = GDN_HEADS * GDN_HEAD
    nb = dim // wb
    assert t % ct == 0 and col0 % wb == 0 and dim % wb == 0 and gate_col % LANES == 0
    cb0 = col0 // wb
    gate_blk = gate_col // LANES

    def col_spec(off):
        return pl.BlockSpec((1, ct, wb), lambda b, h, i: (b, i, cb0 + off + h))

    def cw_spec(off):
        return pl.BlockSpec((CONV_K, wb), lambda b, h, i: (0, off + h))

    full_row = pl.BlockSpec((1, LANES), lambda b, h, i: (0, 0))
    cw = params["conv_w"]
    return pl.pallas_call(
        _gdn_kernel,
        grid=(bsz, nb, t // ct),
        in_specs=[
            col_spec(0), col_spec(nb), col_spec(2 * nb), col_spec(3 * nb),
            pl.BlockSpec((1, ct, LANES), lambda b, h, i: (b, i, gate_blk)),
            cw_spec(0), cw_spec(nb), cw_spec(2 * nb),
            full_row, full_row, full_row,
        ],
        out_specs=pl.BlockSpec((1, ct, wb), lambda b, h, i: (b, i, h)),
        out_shape=jax.ShapeDtypeStruct((bsz, t, dim), BF16),
        scratch_shapes=[pltpu.VMEM((ct + SUBLANES, 3 * wb), F32),
                        pltpu.VMEM((GDN_HEADS_PER_STEP, GDN_HEAD, GDN_HEAD), F32)],
        compiler_params=_cparams(("parallel", "parallel", "arbitrary")),
        name="gated_deltanet_mix",
    )(p, p, p, p, p, cw, cw, cw, params["a_log_row"], params["dt_bias_row"], params["norm_w"])


SSD_CHUNK = 128
SSD_GROUP_CH = 512
SSD_GROUPS_PER_STEP = 8
LOG2E = 1.4426950408889634


def _ssd_kernel(z_ref, x_ref, b_ref, c_ref, dt_ref, cwx_ref, cwb_ref, cwc_ref, cbx_ref, cbb_ref, cbc_ref,
                dtb_ref, alogs_ref, dsk_ref, nw_ref, o_ref, sh_ref, at_ref, st_ref):
    ct = SSD_CHUNK
    gch = SSD_GROUP_CH
    ng = SSD_GROUPS_PER_STEP
    hpg = gch // SSM_HEAD
    xw = ng * gch
    t_idx = pl.program_id(2)

    @pl.when(t_idx == 0)
    def _():
        sh_ref[pl.ds(0, SUBLANES), :] = jnp.zeros((SUBLANES, sh_ref.shape[1]), F32)
        st_ref[...] = jnp.zeros_like(st_ref)

    sh_ref[pl.ds(SUBLANES, ct), 0:xw] = x_ref[0]
    sh_ref[pl.ds(SUBLANES, ct), xw:xw + ng * LANES] = b_ref[0]
    sh_ref[pl.ds(SUBLANES, ct), xw + ng * LANES:xw + 2 * ng * LANES] = c_ref[0]
    xs = _silu(_causal_conv(sh_ref, 0, xw, cwx_ref[...], ct) + cbx_ref[...])
    bm = _silu(_causal_conv(sh_ref, xw, xw + ng * LANES, cwb_ref[...], ct) + cbb_ref[...])
    cm = _silu(_causal_conv(sh_ref, xw + ng * LANES, xw + 2 * ng * LANES, cwc_ref[...], ct) + cbc_ref[...])
    sh_ref[pl.ds(0, SUBLANES), :] = sh_ref[pl.ds(ct, SUBLANES), :]

    dtv = _softplus(dt_ref[0] + dtb_ref[...])
    ri = _iota((ct, ct), 0)
    ci = _iota((ct, ct), 1)
    a_small = dtv * (-jnp.exp(alogs_ref[...]) * LOG2E)
    acum_small = _dot_sel_l(_bf((ri >= ci).astype(F32)), a_small)
    at_ref[...] = acum_small.T
    z = z_ref[0]
    gens = []
    for i in range(ng):
        g_idx = pl.program_id(1) * ng + i
        wide = slice(i * gch, (i + 1) * gch)
        nar = slice(i * LANES, (i + 1) * LANES)
        acum_rows = at_ref[pl.ds(pl.multiple_of(g_idx * hpg, hpg), hpg), :]
        gens.append(_ssd_group(g_idx, xs[:, wide], bm[:, nar], cm[:, nar], z[:, wide], dtv, acum_small, acum_rows,
                               dsk_ref[:, wide], nw_ref[:, wide], st_ref.at[i]))
    for i, out in enumerate(_interleave(gens)):
        o_ref[0, :, i * gch:(i + 1) * gch] = out


def _ssd_group(g_idx, xs, bm, cm, z, dtv, acum_small, acum_rows, dskip, norm_w, st_ref):
    ct = SSD_CHUNK
    gch = SSD_GROUP_CH
    hpg = gch // SSM_HEAD
    expand = _bf((_iota((LANES, gch), 0) == g_idx * hpg + _iota((LANES, gch), 1) // SSM_HEAD).astype(F32))
    both = _dot_sel_r(jnp.concatenate([dtv, acum_small], axis=0), expand)
    yield
    dt_exp = both[:ct]
    acum = both[ct:]
    xdt = xs * dt_exp

    incl = _iota((ct, ct), 0) >= _iota((ct, ct), 1)
    cb = jnp.where(incl, _dot_nt(_bf(cm), _bf(bm)), 0.0)
    st_old = st_ref[...]
    y_state = _dot(_bf(cm), _bf(st_old)) * jnp.exp2(acum)
    yield

    first_half = _iota((ct, LANES), 1) < SSM_HEAD
    y_parts = []
    for jp in range(hpg // 2):
        x_pair = _bf(xdt[:, jp * LANES:(jp + 1) * LANES])
        pair = []
        for j in (2 * jp, 2 * jp + 1):
            col = jnp.broadcast_to(acum[:, j * SSM_HEAD:j * SSM_HEAD + 1], (ct, ct))
            row = jnp.broadcast_to(acum_rows[j:j + 1, :], (ct, ct))
            pair.append(_dot(_bf(cb * jnp.exp2(jnp.minimum(col - row, 0.0))), x_pair))
        y_parts.append(jnp.where(first_half, pair[0], pair[1]))
        yield
    y = jnp.concatenate(y_parts, axis=1) + y_state + xs * dskip

    a_last = acum[ct - 1:ct, :]
    st_ref[...] = st_old * jnp.exp2(a_last) + _dot_tn(_bf(bm), _bf(xdt * jnp.exp2(a_last - acum)))
    yield

    yg = y * _silu(z)
    ms = jnp.mean(yg * yg, axis=-1, keepdims=True)
    return _bf((yg * lax.rsqrt(ms + EPS)) * norm_w)


def _ssd_mix(zx, params):
    bsz, t, _ = zx.shape
    ct = SSD_CHUNK
    gps = SSD_GROUPS_PER_STEP
    xw = gps * SSD_GROUP_CH
    nw = gps * LANES
    d_inner = params["norm_w"].shape[1]
    nsteps = d_inner // xw
    assert t % ct == 0 and d_inner == SSM_GROUPS * SSD_GROUP_CH and d_inner % xw == 0
    xb = d_inner // xw
    bb = 2 * d_inner // nw
    cbk = bb + nsteps
    dtb = (2 * d_inner + 2 * SSM_GROUPS * SSM_STATE) // LANES

    cw = params["conv_w"]
    cbias = params["conv_b"]
    wide = lambda off: pl.BlockSpec((1, ct, xw), lambda b, g, i: (b, i, off + g))
    narrow = lambda off: pl.BlockSpec((1, ct, nw), lambda b, g, i: (b, i, off + g))
    row_wide = pl.BlockSpec((1, xw), lambda b, g, i: (0, g))
    row_full = pl.BlockSpec((1, LANES), lambda b, g, i: (0, 0))
    cb_b = d_inner // nw
    return pl.pallas_call(
        _ssd_kernel,
        grid=(bsz, nsteps, t // ct),
        in_specs=[
            wide(0), wide(xb), narrow(bb), narrow(cbk),
            pl.BlockSpec((1, ct, LANES), lambda b, g, i: (b, i, dtb)),
            pl.BlockSpec((CONV_K, xw), lambda b, g, i: (0, g)),
            pl.BlockSpec((CONV_K, nw), lambda b, g, i: (0, cb_b + g)),
            pl.BlockSpec((CONV_K, nw), lambda b, g, i: (0, cb_b + nsteps + g)),
            pl.BlockSpec((1, xw), lambda b, g, i: (0, g)),
            pl.BlockSpec((1, nw), lambda b, g, i: (0, cb_b + g)),
            pl.BlockSpec((1, nw), lambda b, g, i: (0, cb_b + nsteps + g)),
            row_full, row_full, row_wide, row_wide,
        ],
        out_specs=pl.BlockSpec((1, ct, xw), lambda b, g, i: (b, i, g)),
        out_shape=jax.ShapeDtypeStruct((bsz, t, d_inner), BF16),
        scratch_shapes=[
            pltpu.VMEM((ct + SUBLANES, xw + 2 * nw), F32),
            pltpu.VMEM((LANES, ct), F32),
            pltpu.VMEM((gps, SSM_STATE, SSD_GROUP_CH), F32),
        ],
        compiler_params=_cparams(("parallel", "parallel", "arbitrary")),
        name="mamba2_ssd_mix",
    )(zx, zx, zx, zx, zx, cw, cw, cw, cbias, cbias, cbias, params["dt_bias_row"], params["a_log_row"],
      params["d_exp"], params["norm_w"])


def _pad_cols(w, n):
    return jnp.pad(w, ((0, 0), (0, n - w.shape[1])))


def _lane_row(v, offset):
    return jnp.pad(v, (offset, LANES - offset - v.shape[0])).reshape(1, LANES)


def _mods(mod, bsz, d):
    mod = mod[:bsz]
    return mod[:, None, 0:d], mod[:, None, d:2 * d], mod[:, None, 2 * d:3 * d]


def kernel(x, c, ada_mix_w, ada_mix_b, ada_ffn_w, ada_ffn_b, hg_w_in, hg_w_out, rwkv_mu, rwkv_w0, rwkv_w2,
           rwkv_a0, rwkv_a2, rwkv_g2, rwkv_k_k, rwkv_k_a, rwkv_r_k, rwkv_ln_w, rwkv_ln_b, gdn_conv_w,
           gdn_a_log, gdn_dt_bias, gdn_norm_w, ssm_w_in, ssm_conv_w, ssm_conv_b, ssm_dt_bias, ssm_a_log,
           ssm_d, ssm_norm_w, ssm_w_out, ffn_w1, ffn_w3, ffn_w2, final_norm_w):
    bsz, t, d = x.shape
    depth = ada_mix_w.shape[0]
    assert bsz <= SUBLANES

    c_pad = jnp.pad(c, ((0, SUBLANES - bsz), (0, 0)))
    mix_mod = _adaln(c_pad, ada_mix_w, ada_mix_b)
    ffn_mod = _adaln(c_pad, ada_ffn_w, ada_ffn_b)
    final_w = final_norm_w.reshape(1, d)
    w1_bf, w3_bf, w2_bf = _bf(ffn_w1), _bf(ffn_w3), _bf(ffn_w2)

    for i in range(depth):
        j = i // 2
        shift, scale, gate = _mods(mix_mod[i], bsz, d)
        if i % 2 == 0:
            rwkv_dim = rwkv_w0.shape[1]
            gdn_dim = GDN_HEADS * GDN_HEAD
            rwkv_cols = 3 * rwkv_dim + rwkv_w2.shape[1] + rwkv_a2.shape[1] + rwkv_g2.shape[1]
            w_in = _bf(hg_w_in[j])
            w_in = jnp.concatenate(
                [w_in[:, rwkv_cols:rwkv_cols + 4 * gdn_dim], w_in[:, :rwkv_cols], w_in[:, rwkv_cols + 4 * gdn_dim:]],
                axis=1)
            p = _norm_proj(x, shift, scale, _pad_cols(w_in, 7680), tm=1024, tn=1280)
            row = lambda v: v.reshape(1, -1)
            y_a = _rwkv_mix(p, 4 * gdn_dim, dict(
                mu=row(rwkv_mu[j]), w0=row(rwkv_w0[j]), a0=row(rwkv_a0[j]), k_k=row(rwkv_k_k[j]),
                k_a=row(rwkv_k_a[j]), r_k=row(rwkv_r_k[j]), ln_w=row(rwkv_ln_w[j]), ln_b=row(rwkv_ln_b[j]),
                w2=_bf(rwkv_w2[j]), a2=_bf(rwkv_a2[j]), g2=_bf(rwkv_g2[j])))
            y_b = _gdn_mix(p, 0, 4 * gdn_dim + rwkv_cols, dict(
                conv_w=gdn_conv_w[j], a_log_row=_lane_row(gdn_a_log[j], GDN_HEADS),
                dt_bias_row=_lane_row(gdn_dt_bias[j], GDN_HEADS), norm_w=row(gdn_norm_w[j])))
            w_out = _bf(hg_w_out[j])
            x = _proj_residual([y_a, y_b], [w_out[:rwkv_dim], w_out[rwkv_dim:rwkv_dim + gdn_dim]], x, gate,
                               tm=1024, tn=512)
        else:
            d_inner = ssm_norm_w.shape[1]
            zx = _norm_proj(x, shift, scale, _pad_cols(_bf(ssm_w_in[j]), 10752), tm=1024, tn=1536)
            y = _ssd_mix(zx, dict(
                conv_w=ssm_conv_w[j], conv_b=ssm_conv_b[j].reshape(1, -1),
                dt_bias_row=_lane_row(ssm_dt_bias[j], 0), a_log_row=_lane_row(ssm_a_log[j], 0),
                d_exp=jnp.repeat(ssm_d[j], SSM_HEAD).reshape(1, -1),
                norm_w=ssm_norm_w[j].reshape(1, d_inner)))
            x = _proj_residual([y], [_bf(ssm_w_out[j])], x, gate, tm=1024, tn=512)
        shift, scale, gate = _mods(ffn_mod[i], bsz, d)
        x = _ffn(x, shift, scale, gate, w1_bf, w3_bf, w2_bf, i, final_w, final_norm=(i == depth - 1), tm=512, tf=512)
    return x
```

```python
import functools

import jax
import jax.numpy as jnp
from jax import lax
from jax.experimental import pallas as pl
from jax.experimental.pallas import tpu as pltpu

F32 = jnp.float32
BF16 = jnp.bfloat16

EPS = 1e-5
RWKV_HEAD = 64
RWKV_LN_EPS = 64e-5
GDN_HEAD = 128
GDN_HEADS = 8
SSM_HEAD = 64
SSM_GROUPS = 8
SSM_STATE = 128
CONV_K = 4

LANES = 128
SUBLANES = 8
CHUNK = 64
VMEM_LIMIT = 56 * 1024 * 1024


def _cparams(sem):
    return pltpu.CompilerParams(dimension_semantics=sem, vmem_limit_bytes=VMEM_LIMIT)


def _bf(x):
    return x.astype(BF16)


def _dot(a, b):
    return jnp.dot(a, b, preferred_element_type=F32)


def _dot_nt(a, b):
    return lax.dot_general(a, b, (((1,), (1,)), ((), ())), preferred_element_type=F32)


def _dot_tn(a, b):
    return lax.dot_general(a, b, (((0,), (0,)), ((), ())), preferred_element_type=F32)


def _split2(x):
    hi = _bf(x)
    lo = _bf(x - hi.astype(F32))
    return hi, lo


def _split3(x):
    hi = _bf(x)
    r = x - hi.astype(F32)
    mid = _bf(r)
    lo = _bf(r - mid.astype(F32))
    return hi, mid, lo


def _dot3(a, b):
    ah, al = _split2(a)
    bh, bl = _split2(b)
    return _dot(ah, bh) + (_dot(ah, bl) + _dot(al, bh))


def _dot_sel_l(sel, x):
    hi, mid, lo = _split3(x)
    return _dot(sel, hi) + (_dot(sel, mid) + _dot(sel, lo))


def _dot_sel_r(x, sel):
    hi, mid, lo = _split3(x)
    return _dot(hi, sel) + (_dot(mid, sel) + _dot(lo, sel))


def _silu(x):
    h = 0.5 * x
    return h + h * jnp.tanh(h)


def _softplus(x):
    return jnp.maximum(x, 0.0) + jnp.log(1.0 + jnp.exp(-jnp.abs(x)))


def _iota(shape, dim):
    return lax.broadcasted_iota(jnp.int32, shape, dim)


def _chunk_masks(n, chunk):
    ri = _iota((n, n), 0)
    ci = _iota((n, n), 1)
    same = (ri // chunk) == (ci // chunk)
    return same & (ri >= ci), same & (ri > ci)


INV_BASE = 8


def _unit_lower_inverse(a_strict, chunk):
    n = a_strict.shape[0]
    ri = _iota((n, n), 0)
    ci = _iota((n, n), 1)
    q = jnp.where((ri // INV_BASE) == (ci // INV_BASE), -a_strict, 0.0)
    t = (ri == ci).astype(F32) + q
    for _ in range(2):
        qb = _bf(q)
        q = _dot(qb, qb)
        yield
        t = t + _dot(_bf(t), _bf(q))
        yield
    s = INV_BASE
    while s < chunk:
        off = ((ri // (2 * s)) == (ci // (2 * s))) & ((ri // s) != (ci // s))
        tb = _bf(t)
        x = _dot(_bf(jnp.where(off, a_strict, 0.0)), tb)
        yield
        t = t - _dot(tb, _bf(x))
        yield
        s *= 2
    return t


def _lockstep(gens):
    results = [None] * len(gens)
    live = list(enumerate(gens))
    while live:
        still = []
        for i, gen in live:
            try:
                next(gen)
                still.append((i, gen))
            except StopIteration as done:
                results[i] = done.value
        live = still
        if live:
            yield
    return results


def _interleave(gens):
    runner = _lockstep(gens)
    while True:
        try:
            next(runner)
        except StopIteration as done:
            return done.value


def _rows_bcast(x, rows, chunk):
    n = x.shape[0]
    parts = []
    for c0 in range(0, n, chunk):
        parts.append(jnp.broadcast_to(x[c0 + rows:c0 + rows + 1, :], (chunk, x.shape[1])))
    return jnp.concatenate(parts, axis=0) if len(parts) > 1 else parts[0]


def _adaln_kernel(c_ref, w_ref, b_ref, o_ref):
    o_ref[0] = _dot3(_silu(c_ref[...]), w_ref[0]) + b_ref[0]


def _adaln(c_pad, w, b):
    depth, d, n = w.shape
    tn = 1536
    assert n % tn == 0
    return pl.pallas_call(
        _adaln_kernel,
        grid=(depth, n // tn),
        in_specs=[
            pl.BlockSpec((SUBLANES, d), lambda l, j: (0, 0)),
            pl.BlockSpec((1, d, tn), lambda l, j: (l, 0, j)),
            pl.BlockSpec((1, 1, tn), lambda l, j: (l, 0, j)),
        ],
        out_specs=pl.BlockSpec((1, SUBLANES, tn), lambda l, j: (l, 0, j)),
        out_shape=jax.ShapeDtypeStruct((depth, SUBLANES, n), F32),
        compiler_params=_cparams(("parallel", "parallel")),
        name="adaln_mod",
    )(c_pad, w, b.reshape(depth, 1, n))


def _modulated_norm(x, shift, scale):
    ms = jnp.mean(x * x, axis=-1, keepdims=True)
    return (x * lax.rsqrt(ms + EPS)) * (1.0 + scale) + shift


def _norm_proj_kernel(x_ref, shift_ref, scale_ref, w_ref, o_ref, h_ref):
    @pl.when(pl.program_id(2) == 0)
    def _():
        h_ref[...] = _bf(_modulated_norm(x_ref[0], shift_ref[0], scale_ref[0]))

    o_ref[0] = _dot(h_ref[...], w_ref[...])


def _norm_proj(x, shift, scale, w_bf16, tm, tn):
    bsz, t, d = x.shape
    n = w_bf16.shape[1]
    assert t % tm == 0 and n % tn == 0
    return pl.pallas_call(
        _norm_proj_kernel,
        grid=(bsz, t // tm, n // tn),
        in_specs=[
            pl.BlockSpec((1, tm, d), lambda b, i, j: (b, i, 0)),
            pl.BlockSpec((1, 1, d), lambda b, i, j: (b, 0, 0)),
            pl.BlockSpec((1, 1, d), lambda b, i, j: (b, 0, 0)),
            pl.BlockSpec((d, tn), lambda b, i, j: (0, j)),
        ],
        out_specs=pl.BlockSpec((1, tm, tn), lambda b, i, j: (b, i, j)),
        out_shape=jax.ShapeDtypeStruct((bsz, t, n), F32),
        scratch_shapes=[pltpu.VMEM((tm, d), BF16)],
        compiler_params=_cparams(("parallel", "parallel", "arbitrary")),
        name="norm_proj",
    )(x, shift, scale, w_bf16)


def _proj_residual_kernel(n_in, *refs):
    y_refs = refs[:n_in]
    w_refs = refs[n_in:2 * n_in]
    x_ref, gate_ref, o_ref = refs[2 * n_in:]
    acc = _dot(y_refs[0][0], w_refs[0][...])
    for y_ref, w_ref in zip(y_refs[1:], w_refs[1:]):
        acc = acc + _dot(y_ref[0], w_ref[...])
    o_ref[0] = x_ref[0] + gate_ref[0] * acc


def _proj_residual(ys, ws, x, gate, tm, tn):
    bsz, t, d = x.shape
    n_in = len(ys)
    assert t % tm == 0 and d % tn == 0
    in_specs = [pl.BlockSpec((1, tm, y.shape[2]), lambda b, i, j: (b, i, 0)) for y in ys]
    in_specs += [pl.BlockSpec((w.shape[0], tn), lambda b, i, j: (0, j)) for w in ws]
    in_specs += [
        pl.BlockSpec((1, tm, tn), lambda b, i, j: (b, i, j)),
        pl.BlockSpec((1, 1, tn), lambda b, i, j: (b, 0, j)),
    ]
    return pl.pallas_call(
        functools.partial(_proj_residual_kernel, n_in),
        grid=(bsz, t // tm, d // tn),
        in_specs=in_specs,
        out_specs=pl.BlockSpec((1, tm, tn), lambda b, i, j: (b, i, j)),
        out_shape=jax.ShapeDtypeStruct((bsz, t, d), F32),
        compiler_params=_cparams(("parallel", "parallel", "arbitrary")),
        name="proj_residual",
    )(*ys, *ws, x, gate)


def _ffn_kernel(final_norm, x_ref, shift_ref, scale_ref, gate_ref, w1_ref, w3_ref, w2_ref, fw_ref, o_ref,
                h_ref):
    f = pl.program_id(2)

    @pl.when(f == 0)
    def _():
        h_ref[...] = _bf(_modulated_norm(x_ref[0], shift_ref[0], scale_ref[0]))
        o_ref[...] = jnp.zeros_like(o_ref)

    h = h_ref[...]
    a = _dot(h, w1_ref[...])
    b = _dot(h, w3_ref[...])
    o_ref[0] += _dot(_bf(_silu(a) * b), w2_ref[...])

    @pl.when(f == pl.num_programs(2) - 1)
    def _():
        y = x_ref[0] + gate_ref[0] * o_ref[0]
        if final_norm:
            ms = jnp.mean(y * y, axis=-1, keepdims=True)
            y = (y * lax.rsqrt(ms + EPS)) * fw_ref[...]
        o_ref[0] = y


def _ffn(x, shift, scale, gate, w1, w3, w2, layer, final_w, final_norm, tm, tf):
    bsz, t, d = x.shape
    hidden = w1.shape[2]
    assert t % tm == 0 and hidden % tf == 0
    x_bytes = tm * d * 4
    x_mode = pl.Buffered(1) if 4 * x_bytes > VMEM_LIMIT // 2 else None
    return pl.pallas_call(
        functools.partial(_ffn_kernel, final_norm),
        grid=(bsz, t // tm, hidden // tf),
        in_specs=[
            pl.BlockSpec((1, tm, d), lambda b, i, f: (b, i, 0), pipeline_mode=x_mode),
            pl.BlockSpec((1, 1, d), lambda b, i, f: (b, 0, 0)),
            pl.BlockSpec((1, 1, d), lambda b, i, f: (b, 0, 0)),
            pl.BlockSpec((1, 1, d), lambda b, i, f: (b, 0, 0)),
            pl.BlockSpec((None, d, tf), lambda b, i, f: (layer, 0, f)),
            pl.BlockSpec((None, d, tf), lambda b, i, f: (layer, 0, f)),
            pl.BlockSpec((None, tf, d), lambda b, i, f: (layer, f, 0)),
            pl.BlockSpec((1, d), lambda b, i, f: (0, 0)),
        ],
        out_specs=pl.BlockSpec((1, tm, d), lambda b, i, f: (b, i, 0)),
        out_shape=jax.ShapeDtypeStruct((bsz, t, d), F32),
        scratch_shapes=[pltpu.VMEM((tm, d), BF16)],
        compiler_params=_cparams(("parallel", "parallel", "arbitrary")),
        name="swiglu_ffn",
    )(x, shift, scale, gate, w1, w3, w2, final_w)


RWKV_ROWS = 4 * CHUNK
RWKV_PAIRS = 8


def _head_half_sum(x, first_half):
    s0 = jnp.sum(jnp.where(first_half, x, 0.0), axis=-1, keepdims=True)
    s1 = jnp.sum(jnp.where(first_half, 0.0, x), axis=-1, keepdims=True)
    return jnp.where(first_half, s0, s1)


def _rwkv_kernel(pr_ref, pk_ref, pv_ref, pl_ref, mur_ref, muk_ref, muv_ref, mul_ref, w0_ref, a0_ref, kkw_ref,
                 kaw_ref, rkw_ref, lnw_ref, lnb_ref, w2_ref, a2_ref, g2_ref, o_ref, sh_ref, s_ref):
    ct = RWKV_ROWS
    wb = RWKV_PAIRS * LANES
    t_idx = pl.program_id(2)

    @pl.when(t_idx == 0)
    def _():
        sh_ref[pl.ds(0, SUBLANES), :] = jnp.zeros((SUBLANES, sh_ref.shape[1]), F32)
        s_ref[...] = jnp.zeros_like(s_ref)

    sh_ref[pl.ds(SUBLANES, ct), 0:wb] = pr_ref[0]
    sh_ref[pl.ds(SUBLANES, ct), wb:2 * wb] = pk_ref[0]
    sh_ref[pl.ds(SUBLANES, ct), 2 * wb:3 * wb] = pv_ref[0]
    sh_ref[pl.ds(SUBLANES, ct), 3 * wb:3 * wb + 256] = pl_ref[0]

    def lerp(lo, hi, mu_ref):
        cur = sh_ref[pl.ds(SUBLANES, ct), lo:hi]
        prev = sh_ref[pl.ds(SUBLANES - 1, ct), lo:hi]
        return cur + mu_ref[...] * (prev - cur)

    r = lerp(0, wb, mur_ref)
    k = lerp(wb, 2 * wb, muk_ref)
    v = lerp(2 * wb, 3 * wb, muv_ref)
    xl = lerp(3 * wb, 3 * wb + 256, mul_ref)
    sh_ref[pl.ds(0, SUBLANES), :] = sh_ref[pl.ds(ct, SUBLANES), :]
    pw = xl[:, 0:64]
    pa = xl[:, 64:128]
    pg = xl[:, 128:256]

    w = -_softplus(-(w0_ref[...] + _dot(_bf(jnp.tanh(pw)), w2_ref[...]))) - 0.5
    logd = -jnp.exp(w)
    a = jax.nn.sigmoid(a0_ref[...] + _dot(_bf(pa), a2_ref[...]))
    g = _dot(_bf(jax.nn.sigmoid(pg)), g2_ref[...])

    gens = []
    for p in range(RWKV_PAIRS):
        cols = slice(p * LANES, (p + 1) * LANES)
        gens.append(_rwkv_pair(
            r[:, cols], k[:, cols], v[:, cols], logd[:, cols], a[:, cols], g[:, cols], kkw_ref[:, cols],
            kaw_ref[:, cols], rkw_ref[:, cols], lnw_ref[:, cols], lnb_ref[:, cols], s_ref.at[p]))
    for p, out in enumerate(_interleave(gens)):
        o_ref[0, :, p * LANES:(p + 1) * LANES] = out


def _rwkv_chunk_prep(kkt, rt, bt, kt, kk0, r0, v):
    fh = _iota((CHUNK, LANES), 1) < RWKV_HEAD
    ns = 2 * CHUNK

    def stack(x, masked):
        if masked:
            return jnp.concatenate([jnp.where(fh, x, 0.0), jnp.where(fh, 0.0, x)], axis=0)
        return jnp.concatenate([x, x], axis=0)

    def unstack(x_st):
        return x_st[:CHUNK] + x_st[CHUNK:]

    lhs = _bf(jnp.concatenate([stack(kkt, True), stack(rt, True)], axis=0))
    rhs = _bf(jnp.concatenate([stack(bt, False), stack(kt, False)], axis=0))
    abig = _dot_nt(lhs, rhs)
    yield
    incl, strict = _chunk_masks(ns, CHUNK)
    a_ab = jnp.where(strict, abig[:ns, :ns], 0.0)
    a_ak = jnp.where(strict, abig[:ns, ns:], 0.0)
    a_rb = jnp.where(incl, abig[ns:, :ns], 0.0)
    a_rk = jnp.where(incl, abig[ns:, ns:], 0.0)

    v_st = _bf(stack(v, True))
    av = _dot(_bf(a_ak), v_st)
    ark = _dot(_bf(a_rk), v_st)
    tmat = yield from _unit_lower_inverse(a_ab, CHUNK)
    sol = _dot(_bf(tmat), _bf(jnp.concatenate([stack(kk0, True), av], axis=1)))
    yield
    ar = _dot(_bf(a_rb), _bf(sol))
    yield
    return (unstack(sol[:, :LANES]), unstack(sol[:, LANES:]), r0 - unstack(ar[:, :LANES]),
            unstack(ark - ar[:, LANES:]))


def _rwkv_pair(r, k, v, logd, a, g, kkw, kaw, rkw, lnw, lnb, s_ref):
    ct = RWKV_ROWS
    n_chunks = ct // CHUNK
    first_half = _iota((ct, LANES), 1) < RWKV_HEAD
    kk = k * kkw
    kk = kk * lax.rsqrt(_head_half_sum(kk * kk, first_half) + 1e-24)
    k2 = k * (1.0 + (a - 1.0) * kaw)
    b = kk * a

    incl_ct, _ = _chunk_masks(ct, CHUNK)
    c = _dot_sel_l(_bf(incl_ct.astype(F32)), logd)
    yield
    cm1 = c - logd
    cref = _rows_bcast(c, CHUNK // 2 - 1, CHUNK)
    cend = _rows_bcast(c, CHUNK - 1, CHUNK)
    e_neg = jnp.exp(cref - c)
    rt = r * jnp.exp(c - cref)
    kkt = kk * jnp.exp(cm1 - cref)
    bt = b * e_neg
    kt = k2 * e_neg
    kk0 = kk * jnp.exp(cm1)
    r0 = r * jnp.exp(c)
    e_end = jnp.exp(cend - c)
    bh = b * e_end
    kh = k2 * e_end
    dc = jnp.exp(cend)

    preps = yield from _lockstep([
        _rwkv_chunk_prep(*(x[ci * CHUNK:(ci + 1) * CHUNK] for x in (kkt, rt, bt, kt, kk0, r0, v)))
        for ci in range(n_chunks)])

    ri = _iota((LANES, LANES), 0)
    cj = _iota((LANES, LANES), 1)
    same_head = (ri // RWKV_HEAD) == (cj // RWKV_HEAD)

    ys = []
    for ci in range(n_chunks):
        rows = slice(ci * CHUNK, (ci + 1) * CHUNK)
        wm_c, uv_c, rm_c, yv_c = preps[ci]
        s_old = s_ref[...]
        s_bf = _bf(s_old)
        ys.append(_dot_nt(_bf(rm_c), s_bf) + yv_c)
        m_t = jnp.where(same_head, _dot_tn(_bf(wm_c), _bf(bh[rows])), 0.0)
        hv_t = jnp.where(
            same_head,
            _dot_tn(_bf(jnp.concatenate([v[rows], -uv_c], axis=0)),
                    _bf(jnp.concatenate([kh[rows], bh[rows]], axis=0))),
            0.0)
        yield
        s_ref[...] = s_old * dc[ci * CHUNK:ci * CHUNK + 1, :] - _dot(s_bf, _bf(m_t)) + hv_t
        yield
    y = jnp.concatenate(ys, axis=0)

    inv_n = 1.0 / RWKV_HEAD
    mean = _head_half_sum(y, first_half) * inv_n
    yc = y - mean
    var = _head_half_sum(yc * yc, first_half) * inv_n
    yn = yc * lax.rsqrt(var + RWKV_LN_EPS) * lnw + lnb
    bonus = _head_half_sum(r * k2 * rkw, first_half) * v
    return _bf((yn + bonus) * g)


def _rwkv_mix(p, col0, params):
    bsz, t, _ = p.shape
    dim = params["w0"].shape[1]
    wb = RWKV_PAIRS * LANES
    nb = dim // wb
    ct = RWKV_ROWS
    assert t % ct == 0 and col0 % wb == 0 and (col0 + 3 * dim) % 256 == 0 and dim % wb == 0
    cb0 = col0 // wb
    lb0 = (col0 + 3 * dim) // 256

    def col_spec(off):
        return pl.BlockSpec((1, ct, wb), lambda b, h, i: (b, i, cb0 + off + h))

    def row_spec(off):
        return pl.BlockSpec((1, wb), lambda b, h, i: (0, off + h))

    mu = params["mu"]
    in_specs = [
        col_spec(0), col_spec(nb), col_spec(2 * nb),
        pl.BlockSpec((1, ct, 256), lambda b, h, i: (b, i, lb0)),
        row_spec(0), row_spec(nb), row_spec(2 * nb),
        pl.BlockSpec((1, 256), lambda b, h, i: (0, 3 * dim // 256)),
        row_spec(0), row_spec(0), row_spec(0), row_spec(0), row_spec(0), row_spec(0), row_spec(0),
        pl.BlockSpec((params["w2"].shape[0], wb), lambda b, h, i: (0, h)),
        pl.BlockSpec((params["a2"].shape[0], wb), lambda b, h, i: (0, h)),
        pl.BlockSpec((params["g2"].shape[0], wb), lambda b, h, i: (0, h)),
    ]
    return pl.pallas_call(
        _rwkv_kernel,
        grid=(bsz, nb, t // ct),
        in_specs=in_specs,
        out_specs=pl.BlockSpec((1, ct, wb), lambda b, h, i: (b, i, h)),
        out_shape=jax.ShapeDtypeStruct((bsz, t, dim), BF16),
        scratch_shapes=[pltpu.VMEM((ct + SUBLANES, 3 * wb + 256), F32),
                        pltpu.VMEM((RWKV_PAIRS, LANES, LANES), F32)],
        compiler_params=_cparams(("parallel", "parallel", "arbitrary")),
        name="rwkv7_mix",
    )(p, p, p, p, mu, mu, mu, mu, params["w0"], params["a0"], params["k_k"], params["k_a"], params["r_k"],
      params["ln_w"], params["ln_b"], params["w2"], params["a2"], params["g2"])


GDN_ROWS = 4 * CHUNK
GDN_HEADS_PER_STEP = 8


def _causal_conv(sh_ref, lo, hi, w, n_rows):
    assert CONV_K == 4
    ext = sh_ref[pl.ds(0, n_rows + SUBLANES), lo:hi]
    prev = pltpu.roll(ext, 1, axis=0)
    older = pltpu.roll(prev * w[0:1, :] + ext * w[1:2, :], 2, axis=0)
    return (older + (prev * w[2:3, :] + ext * w[3:4, :]))[SUBLANES:]


def _gdn_kernel(q_ref, k_ref, v_ref, z_ref, gt_ref, cwq_ref, cwk_ref, cwv_ref, alog_ref, dtb_ref, nw_ref,
                o_ref, sh_ref, s_ref):
    ct = GDN_ROWS
    wb = GDN_HEADS_PER_STEP * LANES
    t_idx = pl.program_id(2)

    @pl.when(t_idx == 0)
    def _():
        sh_ref[pl.ds(0, SUBLANES), :] = jnp.zeros((SUBLANES, sh_ref.shape[1]), F32)
        s_ref[...] = jnp.zeros_like(s_ref)

    sh_ref[pl.ds(SUBLANES, ct), 0:wb] = q_ref[0]
    sh_ref[pl.ds(SUBLANES, ct), wb:2 * wb] = k_ref[0]
    sh_ref[pl.ds(SUBLANES, ct), 2 * wb:3 * wb] = v_ref[0]
    q = _silu(_causal_conv(sh_ref, 0, wb, cwq_ref[...], ct))
    k = _silu(_causal_conv(sh_ref, wb, 2 * wb, cwk_ref[...], ct))
    v = _silu(_causal_conv(sh_ref, 2 * wb, 3 * wb, cwv_ref[...], ct))
    sh_ref[pl.ds(0, SUBLANES), :] = sh_ref[pl.ds(ct, SUBLANES), :]

    gates = gt_ref[0]
    beta_all = jax.nn.sigmoid(gates)
    g_all = -jnp.exp(alog_ref[...]) * _softplus(gates + dtb_ref[...])
    z = z_ref[0]
    gens = []
    for p in range(GDN_HEADS_PER_STEP):
        cols = slice(p * LANES, (p + 1) * LANES)
        h_idx = pl.program_id(1) * GDN_HEADS_PER_STEP + p
        gens.append(_gdn_head(q[:, cols], k[:, cols], v[:, cols], z[:, cols], beta_all, g_all, h_idx,
                              nw_ref[...], s_ref.at[p]))
    for p, out in enumerate(_interleave(gens)):
        o_ref[0, :, p * LANES:(p + 1) * LANES] = out


def _gdn_tile_prep(q, k, kb, vb, kbe, gc):
    n = 2 * CHUNK
    incl, strict = _chunk_masks(n, CHUNK)
    decay = jnp.where(incl, jnp.exp(jnp.minimum(gc - gc.T, 0.0)), 0.0)
    kq = _dot_nt(_bf(jnp.concatenate([kb, q], axis=0)), _bf(k))
    yield
    a_mat = jnp.where(strict, kq[:n] * decay, 0.0)
    qk = kq[n:] * decay
    tmat = yield from _unit_lower_inverse(a_mat, CHUNK)
    sol = _dot(_bf(tmat), _bf(jnp.concatenate([vb, kbe], axis=1)))
    yield
    qu = _dot(_bf(qk), _bf(sol))
    yield
    return sol, qu


def _gdn_head(q, k, v, z, beta_all, g_all, h_idx, norm_w, s_ref):
    ct = GDN_ROWS
    n_chunks = ct // CHUNK
    q = q * lax.rsqrt(jnp.sum(q * q, axis=-1, keepdims=True) + 1e-6) * (GDN_HEAD ** -0.5)
    k = k * lax.rsqrt(jnp.sum(k * k, axis=-1, keepdims=True) + 1e-6)
    lane = _iota((ct, LANES), 1)
    beta = jnp.sum(jnp.where(lane == h_idx, beta_all, 0.0), axis=-1, keepdims=True)
    g = jnp.sum(jnp.where(lane == h_idx + GDN_HEADS, g_all, 0.0), axis=-1, keepdims=True)

    incl, _ = _chunk_masks(ct, CHUNK)
    gc = _dot_sel_l(_bf(incl.astype(F32)), jnp.broadcast_to(g, (ct, LANES)))
    yield
    kb = k * beta
    egc = jnp.exp(gc)
    vb = v * beta
    kbe = kb * egc
    tiles = yield from _lockstep([
        _gdn_tile_prep(*(x[t0:t0 + LANES] for x in (q, k, kb, vb, kbe, gc))) for t0 in range(0, ct, LANES)])
    sol = jnp.concatenate([s for s, _ in tiles], axis=0)
    qu = jnp.concatenate([u for _, u in tiles], axis=0)
    o_v = qu[:, :LANES]
    o_m = q * egc - qu[:, LANES:]
    gl = _rows_bcast(gc, CHUNK - 1, CHUNK)
    kd = k * jnp.exp(gl - gc)
    egl = jnp.exp(gl)

    outs = []
    for ci in range(n_chunks):
        rows = slice(ci * CHUNK, (ci + 1) * CHUNK)
        s_old = s_ref[...]
        s_bf = _bf(s_old)
        outs.append(_dot(_bf(o_m[rows]), s_bf) + o_v[rows])
        hm = _dot_tn(_bf(kd[rows]), _bf(sol[rows]))
        yield
        s_ref[...] = s_old * egl[ci * CHUNK:ci * CHUNK + 1, :] - _dot(_bf(hm[:, LANES:]), s_bf) + hm[:, :LANES]
        yield
    o = jnp.concatenate(outs, axis=0)

    ms = jnp.mean(o * o, axis=-1, keepdims=True)
    return _bf((o * lax.rsqrt(ms + EPS)) * norm_w * _silu(z))


def _gdn_mix(p, col0, gate_col, params):
    bsz, t, _ = p.shape
    ct = GDN_ROWS
    wb = GDN_HEADS_PER_STEP * LANES
    dim = GDN_HEADS * GDN_HEAD
    nb = dim // wb
    assert t % ct == 0 and col0 % wb == 0 and dim % wb == 0 and gate_col % LANES == 0
    cb0 = col0 // wb
    gate_blk = gate_col // LANES

    def col_spec(off):
        return pl.BlockSpec((1, ct, wb), lambda b, h, i: (b, i, cb0 + off + h))

    def cw_spec(off):
        return pl.BlockSpec((CONV_K, wb), lambda b, h, i: (0, off + h))

    full_row = pl.BlockSpec((1, LANES), lambda b, h, i: (0, 0))
    cw = params["conv_w"]
    return pl.pallas_call(
        _gdn_kernel,
        grid=(bsz, nb, t // ct),
        in_specs=[
            col_spec(0), col_spec(nb), col_spec(2 * nb), col_spec(3 * nb),
            pl.BlockSpec((1, ct, LANES), lambda b, h, i: (b, i, gate_blk)),
            cw_spec(0), cw_spec(nb), cw_spec(2 * nb),
            full_row, full_row, full_row,
        ],
        out_specs=pl.BlockSpec((1, ct, wb), lambda b, h, i: (b, i, h)),
        out_shape=jax.ShapeDtypeStruct((bsz, t, dim), BF16),
        scratch_shapes=[pltpu.VMEM((ct + SUBLANES, 3 * wb), F32),
                        pltpu.VMEM((GDN_HEADS_PER_STEP, GDN_HEAD, GDN_HEAD), F32)],
        compiler_params=_cparams(("parallel", "parallel", "arbitrary")),
        name="gated_deltanet_mix",
    )(p, p, p, p, p, cw, cw, cw, params["a_log_row"], params["dt_bias_row"], params["norm_w"])


SSD_CHUNK = 128
SSD_GROUP_CH = 512
SSD_GROUPS_PER_STEP = 8
LOG2E = 1.4426950408889634


def _ssd_kernel(z_ref, x_ref, b_ref, c_ref, dt_ref, cwx_ref, cwb_ref, cwc_ref, cbx_ref, cbb_ref, cbc_ref,
                dtb_ref, alogs_ref, dsk_ref, nw_ref, o_ref, sh_ref, at_ref, st_ref):
    ct = SSD_CHUNK
    gch = SSD_GROUP_CH
    ng = SSD_GROUPS_PER_STEP
    hpg = gch // SSM_HEAD
    xw = ng * gch
    t_idx = pl.program_id(2)

    @pl.when(t_idx == 0)
    def _():
        sh_ref[pl.ds(0, SUBLANES), :] = jnp.zeros((SUBLANES, sh_ref.shape[1]), F32)
        st_ref[...] = jnp.zeros_like(st_ref)

    sh_ref[pl.ds(SUBLANES, ct), 0:xw] = x_ref[0]
    sh_ref[pl.ds(SUBLANES, ct), xw:xw + ng * LANES] = b_ref[0]
    sh_ref[pl.ds(SUBLANES, ct), xw + ng * LANES:xw + 2 * ng * LANES] = c_ref[0]
    xs = _silu(_causal_conv(sh_ref, 0, xw, cwx_ref[...], ct) + cbx_ref[...])
    bm = _silu(_causal_conv(sh_ref, xw, xw + ng * LANES, cwb_ref[...], ct) + cbb_ref[...])
    cm = _silu(_causal_conv(sh_ref, xw + ng * LANES, xw + 2 * ng * LANES, cwc_ref[...], ct) + cbc_ref[...])
    sh_ref[pl.ds(0, SUBLANES), :] = sh_ref[pl.ds(ct, SUBLANES), :]

    dtv = _softplus(dt_ref[0] + dtb_ref[...])
    ri = _iota((ct, ct), 0)
    ci = _iota((ct, ct), 1)
    a_small = dtv * (-jnp.exp(alogs_ref[...]) * LOG2E)
    acum_small = _dot_sel_l(_bf((ri >= ci).astype(F32)), a_small)
    at_ref[...] = acum_small.T
    z = z_ref[0]
    gens = []
    for i in range(ng):
        g_idx = pl.program_id(1) * ng + i
        wide = slice(i * gch, (i + 1) * gch)
        nar = slice(i * LANES, (i + 1) * LANES)
        acum_rows = at_ref[pl.ds(pl.multiple_of(g_idx * hpg, hpg), hpg), :]
        gens.append(_ssd_group(g_idx, xs[:, wide], bm[:, nar], cm[:, nar], z[:, wide], dtv, acum_small, acum_rows,
                               dsk_ref[:, wide], nw_ref[:, wide], st_ref.at[i]))
    for i, out in enumerate(_interleave(gens)):
        o_ref[0, :, i * gch:(i + 1) * gch] = out


def _ssd_group(g_idx, xs, bm, cm, z, dtv, acum_small, acum_rows, dskip, norm_w, st_ref):
    ct = SSD_CHUNK
    gch = SSD_GROUP_CH
    hpg = gch // SSM_HEAD
    expand = _bf((_iota((LANES, gch), 0) == g_idx * hpg + _iota((LANES, gch), 1) // SSM_HEAD).astype(F32))
    both = _dot_sel_r(jnp.concatenate([dtv, acum_small], axis=0), expand)
    yield
    dt_exp = both[:ct]
    acum = both[ct:]
    xdt = xs * dt_exp

    incl = _iota((ct, ct), 0) >= _iota((ct, ct), 1)
    cb = jnp.where(incl, _dot_nt(_bf(cm), _bf(bm)), 0.0)
    st_old = st_ref[...]
    y_state = _dot(_bf(cm), _bf(st_old)) * jnp.exp2(acum)
    yield

    first_half = _iota((ct, LANES), 1) < SSM_HEAD
    y_parts = []
    for jp in range(hpg // 2):
        x_pair = _bf(xdt[:, jp * LANES:(jp + 1) * LANES])
        pair = []
        for j in (2 * jp, 2 * jp + 1):
            col = jnp.broadcast_to(acum[:, j * SSM_HEAD:j * SSM_HEAD + 1], (ct, ct))
            row = jnp.broadcast_to(acum_rows[j:j + 1, :], (ct, ct))
            pair.append(_dot(_bf(cb * jnp.exp2(jnp.minimum(col - row, 0.0))), x_pair))
        y_parts.append(jnp.where(first_half, pair[0], pair[1]))
        yield
    y = jnp.concatenate(y_parts, axis=1) + y_state + xs * dskip

    a_last = acum[ct - 1:ct, :]
    st_ref[...] = st_old * jnp.exp2(a_last) + _dot_tn(_bf(bm), _bf(xdt * jnp.exp2(a_last - acum)))
    yield

    yg = y * _silu(z)
    ms = jnp.mean(yg * yg, axis=-1, keepdims=True)
    return _bf((yg * lax.rsqrt(ms + EPS)) * norm_w)


def _ssd_mix(zx, params):
    bsz, t, _ = zx.shape
    ct = SSD_CHUNK
    gps = SSD_GROUPS_PER_STEP
    xw = gps * SSD_GROUP_CH
    nw = gps * LANES
    d_inner = params["norm_w"].shape[1]
    nsteps = d_inner // xw
    assert t % ct == 0 and d_inner == SSM_GROUPS * SSD_GROUP_CH and d_inner % xw == 0
    xb = d_inner // xw
    bb = 2 * d_inner // nw
    cbk = bb + nsteps
    dtb = (2 * d_inner + 2 * SSM_GROUPS * SSM_STATE) // LANES

    cw = params["conv_w"]
    cbias = params["conv_b"]
    wide = lambda off: pl.BlockSpec((1, ct, xw), lambda b, g, i: (b, i, off + g))
    narrow = lambda off: pl.BlockSpec((1, ct, nw), lambda b, g, i: (b, i, off + g))
    row_wide = pl.BlockSpec((1, xw), lambda b, g, i: (0, g))
    row_full = pl.BlockSpec((1, LANES), lambda b, g, i: (0, 0))
    cb_b = d_inner // nw
    return pl.pallas_call(
        _ssd_kernel,
        grid=(bsz, nsteps, t // ct),
        in_specs=[
            wide(0), wide(xb), narrow(bb), narrow(cbk),
            pl.BlockSpec((1, ct, LANES), lambda b, g, i: (b, i, dtb)),
            pl.BlockSpec((CONV_K, xw), lambda b, g, i: (0, g)),
            pl.BlockSpec((CONV_K, nw), lambda b, g, i: (0, cb_b + g)),
            pl.BlockSpec((CONV_K, nw), lambda b, g, i: (0, cb_b + nsteps + g)),
            pl.BlockSpec((1, xw), lambda b, g, i: (0, g)),
            pl.BlockSpec((1, nw), lambda b, g, i: (0, cb_b + g)),
            pl.BlockSpec((1, nw), lambda b, g, i: (0, cb_b + nsteps + g)),
            row_full, row_full, row_wide, row_wide,
        ],
        out_specs=pl.BlockSpec((1, ct, xw), lambda b, g, i: (b, i, g)),
        out_shape=jax.ShapeDtypeStruct((bsz, t, d_inner), BF16),
        scratch_shapes=[
            pltpu.VMEM((ct + SUBLANES, xw + 2 * nw), F32),
            pltpu.VMEM((LANES, ct), F32),
            pltpu.VMEM((gps, SSM_STATE, SSD_GROUP_CH), F32),
        ],
        compiler_params=_cparams(("parallel", "parallel", "arbitrary")),
        name="mamba2_ssd_mix",
    )(zx, zx, zx, zx, zx, cw, cw, cw, cbias, cbias, cbias, params["dt_bias_row"], params["a_log_row"],
      params["d_exp"], params["norm_w"])


def _pad_cols(w, n):
    return jnp.pad(w, ((0, 0), (0, n - w.shape[1])))


def _lane_row(v, offset):
    return jnp.pad(v, (offset, LANES - offset - v.shape[0])).reshape(1, LANES)


def _mods(mod, bsz, d):
    mod = mod[:bsz]
    return mod[:, None, 0:d], mod[:, None, d:2 * d], mod[:, None, 2 * d:3 * d]


def kernel(x, c, ada_mix_w, ada_mix_b, ada_ffn_w, ada_ffn_b, hg_w_in, hg_w_out, rwkv_mu, rwkv_w0, rwkv_w2,
           rwkv_a0, rwkv_a2, rwkv_g2, rwkv_k_k, rwkv_k_a, rwkv_r_k, rwkv_ln_w, rwkv_ln_b, gdn_conv_w,
           gdn_a_log, gdn_dt_bias, gdn_norm_w, ssm_w_in, ssm_conv_w, ssm_conv_b, ssm_dt_bias, ssm_a_log,
           ssm_d, ssm_norm_w, ssm_w_out, ffn_w1, ffn_w3, ffn_w2, final_norm_w):
    bsz, t, d = x.shape
    depth = ada_mix_w.shape[0]
    assert bsz <= SUBLANES

    c_pad = jnp.pad(c, ((0, SUBLANES - bsz), (0, 0)))
    mix_mod = _adaln(c_pad, ada_mix_w, ada_mix_b)
    ffn_mod = _adaln(c_pad, ada_ffn_w, ada_ffn_b)
    final_w = final_norm_w.reshape(1, d)
    w1_bf, w3_bf, w2_bf = _bf(ffn_w1), _bf(ffn_w3), _bf(ffn_w2)

    for i in range(depth):
        j = i // 2
        shift, scale, gate = _mods(mix_mod[i], bsz, d)
        if i % 2 == 0:
            rwkv_dim = rwkv_w0.shape[1]
            gdn_dim = GDN_HEADS * GDN_HEAD
            rwkv_cols = 3 * rwkv_dim + rwkv_w2.shape[1] + rwkv_a2.shape[1] + rwkv_g2.shape[1]
            w_in = _bf(hg_w_in[j])
            w_in = jnp.concatenate(
                [w_in[:, rwkv_cols:rwkv_cols + 4 * gdn_dim], w_in[:, :rwkv_cols], w_in[:, rwkv_cols + 4 * gdn_dim:]],
                axis=1)
            p = _norm_proj(x, shift, scale, _pad_cols(w_in, 7680), tm=1024, tn=1536)
            row = lambda v: v.reshape(1, -1)
            y_a = _rwkv_mix(p, 4 * gdn_dim, dict(
                mu=row(rwkv_mu[j]), w0=row(rwkv_w0[j]), a0=row(rwkv_a0[j]), k_k=row(rwkv_k_k[j]),
                k_a=row(rwkv_k_a[j]), r_k=row(rwkv_r_k[j]), ln_w=row(rwkv_ln_w[j]), ln_b=row(rwkv_ln_b[j]),
                w2=_bf(rwkv_w2[j]), a2=_bf(rwkv_a2[j]), g2=_bf(rwkv_g2[j])))
            y_b = _gdn_mix(p, 0, 4 * gdn_dim + rwkv_cols, dict(
                conv_w=gdn_conv_w[j], a_log_row=_lane_row(gdn_a_log[j], GDN_HEADS),
                dt_bias_row=_lane_row(gdn_dt_bias[j], GDN_HEADS), norm_w=row(gdn_norm_w[j])))
            w_out = _bf(hg_w_out[j])
            x = _proj_residual([y_a, y_b], [w_out[:rwkv_dim], w_out[rwkv_dim:rwkv_dim + gdn_dim]], x, gate,
                               tm=1024, tn=1024)
        else:
            d_inner = ssm_norm_w.shape[1]
            zx = _norm_proj(x, shift, scale, _pad_cols(_bf(ssm_w_in[j]), 10752), tm=1024, tn=1536)
            y = _ssd_mix(zx, dict(
                conv_w=ssm_conv_w[j], conv_b=ssm_conv_b[j].reshape(1, -1),
                dt_bias_row=_lane_row(ssm_dt_bias[j], 0), a_log_row=_lane_row(ssm_a_log[j], 0),
                d_exp=jnp.repeat(ssm_d[j], SSM_HEAD).reshape(1, -1),
                norm_w=ssm_norm_w[j].reshape(1, d_inner)))
            x = _proj_residual([y], [_bf(ssm_w_out[j])], x, gate, tm=512, tn=1024)
        shift, scale, gate = _mods(ffn_mod[i], bsz, d)
        x = _ffn(x, shift, scale, gate, w1_bf, w3_bf, w2_bf, i, final_w, final_norm=(i == depth - 1),
                 tm=1024 if i == 0 else 512, tf=512)
    return x
```

```python
import functools

import jax
import jax.numpy as jnp
from jax import lax
from jax.experimental import pallas as pl
from jax.experimental.pallas import tpu as pltpu

F32 = jnp.float32
BF16 = jnp.bfloat16

EPS = 1e-5
RWKV_HEAD = 64
RWKV_LN_EPS = 64e-5
GDN_HEAD = 128
GDN_HEADS = 8
SSM_HEAD = 64
SSM_GROUPS = 8
SSM_STATE = 128
CONV_K = 4

LANES = 128
SUBLANES = 8
CHUNK = 64
VMEM_LIMIT = 56 * 1024 * 1024


def _cparams(sem):
    return pltpu.CompilerParams(dimension_semantics=sem, vmem_limit_bytes=VMEM_LIMIT)


def _bf(x):
    return x.astype(BF16)


def _dot(a, b):
    return jnp.dot(a, b, preferred_element_type=F32)


def _dot_nt(a, b):
    return lax.dot_general(a, b, (((1,), (1,)), ((), ())), preferred_element_type=F32)


def _dot_tn(a, b):
    return lax.dot_general(a, b, (((0,), (0,)), ((), ())), preferred_element_type=F32)


def _split2(x):
    hi = _bf(x)
    lo = _bf(x - hi.astype(F32))
    return hi, lo


def _split3(x):
    hi = _bf(x)
    r = x - hi.astype(F32)
    mid = _bf(r)
    lo = _bf(r - mid.astype(F32))
    return hi, mid, lo


def _dot3(a, b):
    ah, al = _split2(a)
    bh, bl = _split2(b)
    return _dot(ah, bh) + (_dot(ah, bl) + _dot(al, bh))


def _dot_sel_l(sel, x):
    hi, mid, lo = _split3(x)
    return _dot(sel, hi) + (_dot(sel, mid) + _dot(sel, lo))


def _dot_sel_r(x, sel):
    hi, mid, lo = _split3(x)
    return _dot(hi, sel) + (_dot(mid, sel) + _dot(lo, sel))


def _silu(x):
    h = 0.5 * x
    return h + h * jnp.tanh(h)


def _softplus(x):
    return jnp.maximum(x, 0.0) + jnp.log(1.0 + jnp.exp(-jnp.abs(x)))


def _iota(shape, dim):
    return lax.broadcasted_iota(jnp.int32, shape, dim)


def _chunk_masks(n, chunk):
    ri = _iota((n, n), 0)
    ci = _iota((n, n), 1)
    same = (ri // chunk) == (ci // chunk)
    return same & (ri >= ci), same & (ri > ci)


INV_BASE = 8


def _unit_lower_inverse(a_strict, chunk):
    n = a_strict.shape[0]
    ri = _iota((n, n), 0)
    ci = _iota((n, n), 1)
    q = jnp.where((ri // INV_BASE) == (ci // INV_BASE), -a_strict, 0.0)
    t = (ri == ci).astype(F32) + q
    for _ in range(2):
        qb = _bf(q)
        q = _dot(qb, qb)
        yield
        t = t + _dot(_bf(t), _bf(q))
        yield
    s = INV_BASE
    while s < chunk:
        off = ((ri // (2 * s)) == (ci // (2 * s))) & ((ri // s) != (ci // s))
        tb = _bf(t)
        x = _dot(_bf(jnp.where(off, a_strict, 0.0)), tb)
        yield
        t = t - _dot(tb, _bf(x))
        yield
        s *= 2
    return t


def _lockstep(gens):
    results = [None] * len(gens)
    live = list(enumerate(gens))
    while live:
        still = []
        for i, gen in live:
            try:
                next(gen)
                still.append((i, gen))
            except StopIteration as done:
                results[i] = done.value
        live = still
        if live:
            yield
    return results


def _interleave(gens):
    runner = _lockstep(gens)
    while True:
        try:
            next(runner)
        except StopIteration as done:
            return done.value


def _rows_bcast(x, rows, chunk):
    n = x.shape[0]
    parts = []
    for c0 in range(0, n, chunk):
        parts.append(jnp.broadcast_to(x[c0 + rows:c0 + rows + 1, :], (chunk, x.shape[1])))
    return jnp.concatenate(parts, axis=0) if len(parts) > 1 else parts[0]


def _adaln_kernel(c_ref, w_ref, b_ref, o_ref):
    o_ref[0] = _dot3(_silu(c_ref[...]), w_ref[0]) + b_ref[0]


def _adaln(c_pad, w, b):
    depth, d, n = w.shape
    tn = 1536
    assert n % tn == 0
    return pl.pallas_call(
        _adaln_kernel,
        grid=(depth, n // tn),
        in_specs=[
            pl.BlockSpec((SUBLANES, d), lambda l, j: (0, 0)),
            pl.BlockSpec((1, d, tn), lambda l, j: (l, 0, j)),
            pl.BlockSpec((1, 1, tn), lambda l, j: (l, 0, j)),
        ],
        out_specs=pl.BlockSpec((1, SUBLANES, tn), lambda l, j: (l, 0, j)),
        out_shape=jax.ShapeDtypeStruct((depth, SUBLANES, n), F32),
        compiler_params=_cparams(("parallel", "parallel")),
        name="adaln_mod",
    )(c_pad, w, b.reshape(depth, 1, n))


def _modulated_norm(x, shift, scale):
    ms = jnp.mean(x * x, axis=-1, keepdims=True)
    return (x * lax.rsqrt(ms + EPS)) * (1.0 + scale) + shift


def _norm_proj_kernel(x_ref, shift_ref, scale_ref, w_ref, o_ref, h_ref):
    @pl.when(pl.program_id(2) == 0)
    def _():
        h_ref[...] = _bf(_modulated_norm(x_ref[0], shift_ref[0], scale_ref[0]))

    o_ref[0] = _dot(h_ref[...], w_ref[...])


def _norm_proj(x, shift, scale, w_bf16, tm, tn):
    bsz, t, d = x.shape
    n = w_bf16.shape[1]
    assert t % tm == 0 and n % tn == 0
    return pl.pallas_call(
        _norm_proj_kernel,
        grid=(bsz, t // tm, n // tn),
        in_specs=[
            pl.BlockSpec((1, tm, d), lambda b, i, j: (b, i, 0)),
            pl.BlockSpec((1, 1, d), lambda b, i, j: (b, 0, 0)),
            pl.BlockSpec((1, 1, d), lambda b, i, j: (b, 0, 0)),
            pl.BlockSpec((d, tn), lambda b, i, j: (0, j)),
        ],
        out_specs=pl.BlockSpec((1, tm, tn), lambda b, i, j: (b, i, j)),
        out_shape=jax.ShapeDtypeStruct((bsz, t, n), F32),
        scratch_shapes=[pltpu.VMEM((tm, d), BF16)],
        compiler_params=_cparams(("parallel", "parallel", "arbitrary")),
        name="norm_proj",
    )(x, shift, scale, w_bf16)


def _proj_residual_kernel(n_in, *refs):
    y_refs = refs[:n_in]
    w_refs = refs[n_in:2 * n_in]
    x_ref, gate_ref, o_ref = refs[2 * n_in:]
    acc = _dot(y_refs[0][0], w_refs[0][...])
    for y_ref, w_ref in zip(y_refs[1:], w_refs[1:]):
        acc = acc + _dot(y_ref[0], w_ref[...])
    o_ref[0] = x_ref[0] + gate_ref[0] * acc


def _proj_residual(ys, ws, x, gate, tm, tn):
    bsz, t, d = x.shape
    n_in = len(ys)
    assert t % tm == 0 and d % tn == 0
    in_specs = [pl.BlockSpec((1, tm, y.shape[2]), lambda b, i, j: (b, i, 0)) for y in ys]
    in_specs += [pl.BlockSpec((w.shape[0], tn), lambda b, i, j: (0, j)) for w in ws]
    in_specs += [
        pl.BlockSpec((1, tm, tn), lambda b, i, j: (b, i, j)),
        pl.BlockSpec((1, 1, tn), lambda b, i, j: (b, 0, j)),
    ]
    return pl.pallas_call(
        functools.partial(_proj_residual_kernel, n_in),
        grid=(bsz, t // tm, d // tn),
        in_specs=in_specs,
        out_specs=pl.BlockSpec((1, tm, tn), lambda b, i, j: (b, i, j)),
        out_shape=jax.ShapeDtypeStruct((bsz, t, d), F32),
        compiler_params=_cparams(("parallel", "parallel", "arbitrary")),
        name="proj_residual",
    )(*ys, *ws, x, gate)


def _ffn_kernel(final_norm, x_ref, shift_ref, scale_ref, gate_ref, w1_ref, w3_ref, w2_ref, fw_ref, o_ref,
                h_ref):
    f = pl.program_id(2)

    @pl.when(f == 0)
    def _():
        h_ref[...] = _bf(_modulated_norm(x_ref[0], shift_ref[0], scale_ref[0]))
        o_ref[...] = jnp.zeros_like(o_ref)

    h = h_ref[...]
    a = _dot(h, w1_ref[...])
    b = _dot(h, w3_ref[...])
    o_ref[0] += _dot(_bf(_silu(a) * b), w2_ref[...])

    @pl.when(f == pl.num_programs(2) - 1)
    def _():
        y = x_ref[0] + gate_ref[0] * o_ref[0]
        if final_norm:
            ms = jnp.mean(y * y, axis=-1, keepdims=True)
            y = (y * lax.rsqrt(ms + EPS)) * fw_ref[...]
        o_ref[0] = y


def _ffn(x, shift, scale, gate, w1, w3, w2, layer, final_w, final_norm, tm, tf):
    bsz, t, d = x.shape
    hidden = w1.shape[2]
    assert t % tm == 0 and hidden % tf == 0
    return pl.pallas_call(
        functools.partial(_ffn_kernel, final_norm),
        grid=(bsz, t // tm, hidden // tf),
        in_specs=[
            pl.BlockSpec((1, tm, d), lambda b, i, f: (b, i, 0)),
            pl.BlockSpec((1, 1, d), lambda b, i, f: (b, 0, 0)),
            pl.BlockSpec((1, 1, d), lambda b, i, f: (b, 0, 0)),
            pl.BlockSpec((1, 1, d), lambda b, i, f: (b, 0, 0)),
            pl.BlockSpec((None, d, tf), lambda b, i, f: (layer, 0, f)),
            pl.BlockSpec((None, d, tf), lambda b, i, f: (layer, 0, f)),
            pl.BlockSpec((None, tf, d), lambda b, i, f: (layer, f, 0)),
            pl.BlockSpec((1, d), lambda b, i, f: (0, 0)),
        ],
        out_specs=pl.BlockSpec((1, tm, d), lambda b, i, f: (b, i, 0)),
        out_shape=jax.ShapeDtypeStruct((bsz, t, d), F32),
        scratch_shapes=[pltpu.VMEM((tm, d), BF16)],
        compiler_params=_cparams(("parallel", "parallel", "arbitrary")),
        name="swiglu_ffn",
    )(x, shift, scale, gate, w1, w3, w2, final_w)


RWKV_ROWS = 4 * CHUNK
RWKV_PAIRS = 8


def _head_half_sum(x, first_half):
    s0 = jnp.sum(jnp.where(first_half, x, 0.0), axis=-1, keepdims=True)
    s1 = jnp.sum(jnp.where(first_half, 0.0, x), axis=-1, keepdims=True)
    return jnp.where(first_half, s0, s1)


def _rwkv_kernel(pr_ref, pk_ref, pv_ref, pl_ref, mur_ref, muk_ref, muv_ref, mul_ref, w0_ref, a0_ref, kkw_ref,
                 kaw_ref, rkw_ref, lnw_ref, lnb_ref, w2_ref, a2_ref, g2_ref, o_ref, sh_ref, s_ref):
    ct = RWKV_ROWS
    wb = RWKV_PAIRS * LANES
    t_idx = pl.program_id(2)

    @pl.when(t_idx == 0)
    def _():
        sh_ref[pl.ds(0, SUBLANES), :] = jnp.zeros((SUBLANES, sh_ref.shape[1]), F32)
        s_ref[...] = jnp.zeros_like(s_ref)

    sh_ref[pl.ds(SUBLANES, ct), 0:wb] = pr_ref[0]
    sh_ref[pl.ds(SUBLANES, ct), wb:2 * wb] = pk_ref[0]
    sh_ref[pl.ds(SUBLANES, ct), 2 * wb:3 * wb] = pv_ref[0]
    sh_ref[pl.ds(SUBLANES, ct), 3 * wb:3 * wb + 256] = pl_ref[0]

    def lerp(lo, hi, mu_ref):
        cur = sh_ref[pl.ds(SUBLANES, ct), lo:hi]
        prev = sh_ref[pl.ds(SUBLANES - 1, ct), lo:hi]
        return cur + mu_ref[...] * (prev - cur)

    r = lerp(0, wb, mur_ref)
    k = lerp(wb, 2 * wb, muk_ref)
    v = lerp(2 * wb, 3 * wb, muv_ref)
    xl = lerp(3 * wb, 3 * wb + 256, mul_ref)
    sh_ref[pl.ds(0, SUBLANES), :] = sh_ref[pl.ds(ct, SUBLANES), :]
    pw = xl[:, 0:64]
    pa = xl[:, 64:128]
    pg = xl[:, 128:256]

    w = -_softplus(-(w0_ref[...] + _dot(_bf(jnp.tanh(pw)), w2_ref[...]))) - 0.5
    logd = -jnp.exp(w)
    a = jax.nn.sigmoid(a0_ref[...] + _dot(_bf(pa), a2_ref[...]))
    g = _dot(_bf(jax.nn.sigmoid(pg)), g2_ref[...])

    gens = []
    for p in range(RWKV_PAIRS):
        cols = slice(p * LANES, (p + 1) * LANES)
        gens.append(_rwkv_pair(
            r[:, cols], k[:, cols], v[:, cols], logd[:, cols], a[:, cols], g[:, cols], kkw_ref[:, cols],
            kaw_ref[:, cols], rkw_ref[:, cols], lnw_ref[:, cols], lnb_ref[:, cols], s_ref.at[p]))
    for p, out in enumerate(_interleave(gens)):
        o_ref[0, :, p * LANES:(p + 1) * LANES] = out


def _rwkv_chunk_prep(kkt, rt, bt, kt, kk0, r0, v):
    fh = _iota((CHUNK, LANES), 1) < RWKV_HEAD
    ns = 2 * CHUNK

    def stack(x, masked):
        if masked:
            return jnp.concatenate([jnp.where(fh, x, 0.0), jnp.where(fh, 0.0, x)], axis=0)
        return jnp.concatenate([x, x], axis=0)

    def unstack(x_st):
        return x_st[:CHUNK] + x_st[CHUNK:]

    lhs = _bf(jnp.concatenate([stack(kkt, True), stack(rt, True)], axis=0))
    rhs = _bf(jnp.concatenate([stack(bt, False), stack(kt, False)], axis=0))
    abig = _dot_nt(lhs, rhs)
    yield
    incl, strict = _chunk_masks(ns, CHUNK)
    a_ab = jnp.where(strict, abig[:ns, :ns], 0.0)
    a_ak = jnp.where(strict, abig[:ns, ns:], 0.0)
    a_rb = jnp.where(incl, abig[ns:, :ns], 0.0)
    a_rk = jnp.where(incl, abig[ns:, ns:], 0.0)

    v_st = _bf(stack(v, True))
    av = _dot(_bf(a_ak), v_st)
    ark = _dot(_bf(a_rk), v_st)
    tmat = yield from _unit_lower_inverse(a_ab, CHUNK)
    sol = _dot(_bf(tmat), _bf(jnp.concatenate([stack(kk0, True), av], axis=1)))
    yield
    ar = _dot(_bf(a_rb), _bf(sol))
    yield
    return (unstack(sol[:, :LANES]), unstack(sol[:, LANES:]), r0 - unstack(ar[:, :LANES]),
            unstack(ark - ar[:, LANES:]))


def _rwkv_pair(r, k, v, logd, a, g, kkw, kaw, rkw, lnw, lnb, s_ref):
    ct = RWKV_ROWS
    n_chunks = ct // CHUNK
    first_half = _iota((ct, LANES), 1) < RWKV_HEAD
    kk = k * kkw
    kk = kk * lax.rsqrt(_head_half_sum(kk * kk, first_half) + 1e-24)
    k2 = k * (1.0 + (a - 1.0) * kaw)
    b = kk * a

    incl_ct, _ = _chunk_masks(ct, CHUNK)
    c = _dot_sel_l(_bf(incl_ct.astype(F32)), logd)
    yield
    cm1 = c - logd
    cref = _rows_bcast(c, CHUNK // 2 - 1, CHUNK)
    cend = _rows_bcast(c, CHUNK - 1, CHUNK)
    e_neg = jnp.exp(cref - c)
    rt = r * jnp.exp(c - cref)
    kkt = kk * jnp.exp(cm1 - cref)
    bt = b * e_neg
    kt = k2 * e_neg
    kk0 = kk * jnp.exp(cm1)
    r0 = r * jnp.exp(c)
    e_end = jnp.exp(cend - c)
    bh = b * e_end
    kh = k2 * e_end
    dc = jnp.exp(cend)

    preps = yield from _lockstep([
        _rwkv_chunk_prep(*(x[ci * CHUNK:(ci + 1) * CHUNK] for x in (kkt, rt, bt, kt, kk0, r0, v)))
        for ci in range(n_chunks)])

    ri = _iota((LANES, LANES), 0)
    cj = _iota((LANES, LANES), 1)
    same_head = (ri // RWKV_HEAD) == (cj // RWKV_HEAD)

    ys = []
    for ci in range(n_chunks):
        rows = slice(ci * CHUNK, (ci + 1) * CHUNK)
        wm_c, uv_c, rm_c, yv_c = preps[ci]
        s_old = s_ref[...]
        s_bf = _bf(s_old)
        ys.append(_dot_nt(_bf(rm_c), s_bf) + yv_c)
        m_t = jnp.where(same_head, _dot_tn(_bf(wm_c), _bf(bh[rows])), 0.0)
        hv_t = jnp.where(
            same_head,
            _dot_tn(_bf(jnp.concatenate([v[rows], -uv_c], axis=0)),
                    _bf(jnp.concatenate([kh[rows], bh[rows]], axis=0))),
            0.0)
        yield
        s_ref[...] = s_old * dc[ci * CHUNK:ci * CHUNK + 1, :] - _dot(s_bf, _bf(m_t)) + hv_t
        yield
    y = jnp.concatenate(ys, axis=0)

    inv_n = 1.0 / RWKV_HEAD
    mean = _head_half_sum(y, first_half) * inv_n
    yc = y - mean
    var = _head_half_sum(yc * yc, first_half) * inv_n
    yn = yc * lax.rsqrt(var + RWKV_LN_EPS) * lnw + lnb
    bonus = _head_half_sum(r * k2 * rkw, first_half) * v
    return _bf((yn + bonus) * g)


def _rwkv_mix(p, col0, params):
    bsz, t, _ = p.shape
    dim = params["w0"].shape[1]
    wb = RWKV_PAIRS * LANES
    nb = dim // wb
    ct = RWKV_ROWS
    assert t % ct == 0 and col0 % wb == 0 and (col0 + 3 * dim) % 256 == 0 and dim % wb == 0
    cb0 = col0 // wb
    lb0 = (col0 + 3 * dim) // 256

    def col_spec(off):
        return pl.BlockSpec((1, ct, wb), lambda b, h, i: (b, i, cb0 + off + h))

    def row_spec(off):
        return pl.BlockSpec((1, wb), lambda b, h, i: (0, off + h))

    mu = params["mu"]
    in_specs = [
        col_spec(0), col_spec(nb), col_spec(2 * nb),
        pl.BlockSpec((1, ct, 256), lambda b, h, i: (b, i, lb0)),
        row_spec(0), row_spec(nb), row_spec(2 * nb),
        pl.BlockSpec((1, 256), lambda b, h, i: (0, 3 * dim // 256)),
        row_spec(0), row_spec(0), row_spec(0), row_spec(0), row_spec(0), row_spec(0), row_spec(0),
        pl.BlockSpec((params["w2"].shape[0], wb), lambda b, h, i: (0, h)),
        pl.BlockSpec((params["a2"].shape[0], wb), lambda b, h, i: (0, h)),
        pl.BlockSpec((params["g2"].shape[0], wb), lambda b, h, i: (0, h)),
    ]
    return pl.pallas_call(
        _rwkv_kernel,
        grid=(bsz, nb, t // ct),
        in_specs=in_specs,
        out_specs=pl.BlockSpec((1, ct, wb), lambda b, h, i: (b, i, h)),
        out_shape=jax.ShapeDtypeStruct((bsz, t, dim), BF16),
        scratch_shapes=[pltpu.VMEM((ct + SUBLANES, 3 * wb + 256), F32),
                        pltpu.VMEM((RWKV_PAIRS, LANES, LANES), F32)],
        compiler_params=_cparams(("parallel", "parallel", "arbitrary")),
        name="rwkv7_mix",
    )(p, p, p, p, mu, mu, mu, mu, params["w0"], params["a0"], params["k_k"], params["k_a"], params["r_k"],
      params["ln_w"], params["ln_b"], params["w2"], params["a2"], params["g2"])


GDN_ROWS = 4 * CHUNK
GDN_HEADS_PER_STEP = 8


def _causal_conv(sh_ref, lo, hi, w, n_rows):
    assert CONV_K == 4
    ext = sh_ref[pl.ds(0, n_rows + SUBLANES), lo:hi]
    prev = pltpu.roll(ext, 1, axis=0)
    older = pltpu.roll(prev * w[0:1, :] + ext * w[1:2, :], 2, axis=0)
    return (older + (prev * w[2:3, :] + ext * w[3:4, :]))[SUBLANES:]


def _gdn_kernel(q_ref, k_ref, v_ref, z_ref, gt_ref, cwq_ref, cwk_ref, cwv_ref, alog_ref, dtb_ref, nw_ref,
                o_ref, sh_ref, s_ref):
    ct = GDN_ROWS
    wb = GDN_HEADS_PER_STEP * LANES
    t_idx = pl.program_id(2)

    @pl.when(t_idx == 0)
    def _():
        sh_ref[pl.ds(0, SUBLANES), :] = jnp.zeros((SUBLANES, sh_ref.shape[1]), F32)
        s_ref[...] = jnp.zeros_like(s_ref)

    sh_ref[pl.ds(SUBLANES, ct), 0:wb] = q_ref[0]
    sh_ref[pl.ds(SUBLANES, ct), wb:2 * wb] = k_ref[0]
    sh_ref[pl.ds(SUBLANES, ct), 2 * wb:3 * wb] = v_ref[0]
    q = _silu(_causal_conv(sh_ref, 0, wb, cwq_ref[...], ct))
    k = _silu(_causal_conv(sh_ref, wb, 2 * wb, cwk_ref[...], ct))
    v = _silu(_causal_conv(sh_ref, 2 * wb, 3 * wb, cwv_ref[...], ct))
    sh_ref[pl.ds(0, SUBLANES), :] = sh_ref[pl.ds(ct, SUBLANES), :]

    gates = gt_ref[0]
    beta_all = jax.nn.sigmoid(gates)
    g_all = -jnp.exp(alog_ref[...]) * _softplus(gates + dtb_ref[...])
    z = z_ref[0]
    gens = []
    for p in range(GDN_HEADS_PER_STEP):
        cols = slice(p * LANES, (p + 1) * LANES)
        h_idx = pl.program_id(1) * GDN_HEADS_PER_STEP + p
        gens.append(_gdn_head(q[:, cols], k[:, cols], v[:, cols], z[:, cols], beta_all, g_all, h_idx,
                              nw_ref[...], s_ref.at[p]))
    for p, out in enumerate(_interleave(gens)):
        o_ref[0, :, p * LANES:(p + 1) * LANES] = out


def _gdn_tile_prep(q, k, kb, vb, kbe, gc):
    n = 2 * CHUNK
    incl, strict = _chunk_masks(n, CHUNK)
    decay = jnp.where(incl, jnp.exp(jnp.minimum(gc - gc.T, 0.0)), 0.0)
    kq = _dot_nt(_bf(jnp.concatenate([kb, q], axis=0)), _bf(k))
    yield
    a_mat = jnp.where(strict, kq[:n] * decay, 0.0)
    qk = kq[n:] * decay
    tmat = yield from _unit_lower_inverse(a_mat, CHUNK)
    sol = _dot(_bf(tmat), _bf(jnp.concatenate([vb, kbe], axis=1)))
    yield
    qu = _dot(_bf(qk), _bf(sol))
    yield
    return sol, qu


def _gdn_head(q, k, v, z, beta_all, g_all, h_idx, norm_w, s_ref):
    ct = GDN_ROWS
    n_chunks = ct // CHUNK
    q = q * lax.rsqrt(jnp.sum(q * q, axis=-1, keepdims=True) + 1e-6) * (GDN_HEAD ** -0.5)
    k = k * lax.rsqrt(jnp.sum(k * k, axis=-1, keepdims=True) + 1e-6)
    lane = _iota((ct, LANES), 1)
    beta = jnp.sum(jnp.where(lane == h_idx, beta_all, 0.0), axis=-1, keepdims=True)
    g = jnp.sum(jnp.where(lane == h_idx + GDN_HEADS, g_all, 0.0), axis=-1, keepdims=True)

    incl, _ = _chunk_masks(ct, CHUNK)
    gc = _dot_sel_l(_bf(incl.astype(F32)), jnp.broadcast_to(g, (ct, LANES)))
    yield
    kb = k * beta
    egc = jnp.exp(gc)
    vb = v * beta
    kbe = kb * egc
    tiles = yield from _lockstep([
        _gdn_tile_prep(*(x[t0:t0 + LANES] for x in (q, k, kb, vb, kbe, gc))) for t0 in range(0, ct, LANES)])
    sol = jnp.concatenate([s for s, _ in tiles], axis=0)
    qu = jnp.concatenate([u for _, u in tiles], axis=0)
    o_v = qu[:, :LANES]
    o_m = q * egc - qu[:, LANES:]
    gl = _rows_bcast(gc, CHUNK - 1, CHUNK)
    kd = k * jnp.exp(gl - gc)
    egl = jnp.exp(gl)

    outs = []
    for ci in range(n_chunks):
        rows = slice(ci * CHUNK, (ci + 1) * CHUNK)
        s_old = s_ref[...]
        s_bf = _bf(s_old)
        outs.append(_dot(_bf(o_m[rows]), s_bf) + o_v[rows])
        hm = _dot_tn(_bf(kd[rows]), _bf(sol[rows]))
        yield
        s_ref[...] = s_old * egl[ci * CHUNK:ci * CHUNK + 1, :] - _dot(_bf(hm[:, LANES:]), s_bf) + hm[:, :LANES]
        yield
    o = jnp.concatenate(outs, axis=0)

    ms = jnp.mean(o * o, axis=-1, keepdims=True)
    return _bf((o * lax.rsqrt(ms + EPS)) * norm_w * _silu(z))


def _gdn_mix(p, col0, gate_col, params):
    bsz, t, _ = p.shape
    ct = GDN_ROWS
    wb = GDN_HEADS_PER_STEP * LANES
    dim = GDN_HEADS * GDN_HEAD
    nb = dim // wb
    assert t % ct == 0 and col0 % wb == 0 and dim % wb == 0 and gate_col % LANES == 0
    cb0 = col0 // wb
    gate_blk = gate_col // LANES

    def col_spec(off):
        return pl.BlockSpec((1, ct, wb), lambda b, h, i: (b, i, cb0 + off + h))

    def cw_spec(off):
        return pl.BlockSpec((CONV_K, wb), lambda b, h, i: (0, off + h))

    full_row = pl.BlockSpec((1, LANES), lambda b, h, i: (0, 0))
    cw = params["conv_w"]
    return pl.pallas_call(
        _gdn_kernel,
        grid=(bsz, nb, t // ct),
        in_specs=[
            col_spec(0), col_spec(nb), col_spec(2 * nb), col_spec(3 * nb),
            pl.BlockSpec((1, ct, LANES), lambda b, h, i: (b, i, gate_blk)),
            cw_spec(0), cw_spec(nb), cw_spec(2 * nb),
            full_row, full_row, full_row,
        ],
        out_specs=pl.BlockSpec((1, ct, wb), lambda b, h, i: (b, i, h)),
        out_shape=jax.ShapeDtypeStruct((bsz, t, dim), BF16),
        scratch_shapes=[pltpu.VMEM((ct + SUBLANES, 3 * wb), F32),
                        pltpu.VMEM((GDN_HEADS_PER_STEP, GDN_HEAD, GDN_HEAD), F32)],
        compiler_params=_cparams(("parallel", "parallel", "arbitrary")),
        name="gated_deltanet_mix",
    )(p, p, p, p, p, cw, cw, cw, params["a_log_row"], params["dt_bias_row"], params["norm_w"])


SSD_CHUNK = 128
SSD_GROUP_CH = 512
SSD_GROUPS_PER_STEP = 8
LOG2E = 1.4426950408889634


def _ssd_kernel(z_ref, x_ref, b_ref, c_ref, dt_ref, cwx_ref, cwb_ref, cwc_ref, cbx_ref, cbb_ref, cbc_ref,
                dtb_ref, alogs_ref, dsk_ref, nw_ref, o_ref, sh_ref, at_ref, st_ref):
    ct = SSD_CHUNK
    gch = SSD_GROUP_CH
    ng = SSD_GROUPS_PER_STEP
    hpg = gch // SSM_HEAD
    xw = ng * gch
    t_idx = pl.program_id(2)

    @pl.when(t_idx == 0)
    def _():
        sh_ref[pl.ds(0, SUBLANES), :] = jnp.zeros((SUBLANES, sh_ref.shape[1]), F32)
        st_ref[...] = jnp.zeros_like(st_ref)

    sh_ref[pl.ds(SUBLANES, ct), 0:xw] = x_ref[0]
    sh_ref[pl.ds(SUBLANES, ct), xw:xw + ng * LANES] = b_ref[0]
    sh_ref[pl.ds(SUBLANES, ct), xw + ng * LANES:xw + 2 * ng * LANES] = c_ref[0]
    xs = _silu(_causal_conv(sh_ref, 0, xw, cwx_ref[...], ct) + cbx_ref[...])
    bm = _silu(_causal_conv(sh_ref, xw, xw + ng * LANES, cwb_ref[...], ct) + cbb_ref[...])
    cm = _silu(_causal_conv(sh_ref, xw + ng * LANES, xw + 2 * ng * LANES, cwc_ref[...], ct) + cbc_ref[...])
    sh_ref[pl.ds(0, SUBLANES), :] = sh_ref[pl.ds(ct, SUBLANES), :]

    dtv = _softplus(dt_ref[0] + dtb_ref[...])
    ri = _iota((ct, ct), 0)
    ci = _iota((ct, ct), 1)
    a_small = dtv * (-jnp.exp(alogs_ref[...]) * LOG2E)
    acum_small = _dot_sel_l(_bf((ri >= ci).astype(F32)), a_small)
    at_ref[...] = acum_small.T
    head_cols = _split3(jnp.concatenate([dtv, acum_small], axis=0))
    z = z_ref[0]
    gens = []
    for i in range(ng):
        g_idx = pl.program_id(1) * ng + i
        wide = slice(i * gch, (i + 1) * gch)
        nar = slice(i * LANES, (i + 1) * LANES)
        acum_rows = at_ref[pl.ds(pl.multiple_of(g_idx * hpg, hpg), hpg), :]
        gens.append(_ssd_group(g_idx, xs[:, wide], bm[:, nar], cm[:, nar], z[:, wide], head_cols, acum_rows,
                               dsk_ref[:, wide], nw_ref[:, wide], st_ref.at[i]))
    for i, out in enumerate(_interleave(gens)):
        o_ref[0, :, i * gch:(i + 1) * gch] = out


def _ssd_group(g_idx, xs, bm, cm, z, head_cols, acum_rows, dskip, norm_w, st_ref):
    ct = SSD_CHUNK
    gch = SSD_GROUP_CH
    hpg = gch // SSM_HEAD
    expand = _bf((_iota((LANES, gch), 0) == g_idx * hpg + _iota((LANES, gch), 1) // SSM_HEAD).astype(F32))
    hi, mid, lo = head_cols
    both = _dot(hi, expand) + (_dot(mid, expand) + _dot(lo, expand))
    yield
    dt_exp = both[:ct]
    acum = both[ct:]
    xdt = xs * dt_exp

    incl = _iota((ct, ct), 0) >= _iota((ct, ct), 1)
    cb = jnp.where(incl, _dot_nt(_bf(cm), _bf(bm)), 0.0)
    st_old = st_ref[...]
    y_state = _dot(_bf(cm), _bf(st_old)) * jnp.exp2(acum)
    yield

    first_half = _iota((ct, LANES), 1) < SSM_HEAD
    y_parts = []
    for jp in range(hpg // 2):
        x_pair = _bf(xdt[:, jp * LANES:(jp + 1) * LANES])
        pair = []
        for j in (2 * jp, 2 * jp + 1):
            col = jnp.broadcast_to(acum[:, j * SSM_HEAD:j * SSM_HEAD + 1], (ct, ct))
            row = jnp.broadcast_to(acum_rows[j:j + 1, :], (ct, ct))
            pair.append(_dot(_bf(cb * jnp.exp2(jnp.minimum(col - row, 0.0))), x_pair))
        y_parts.append(jnp.where(first_half, pair[0], pair[1]))
        yield
    y = jnp.concatenate(y_parts, axis=1) + y_state + xs * dskip

    a_last = acum[ct - 1:ct, :]
    st_ref[...] = st_old * jnp.exp2(a_last) + _dot_tn(_bf(bm), _bf(xdt * jnp.exp2(a_last - acum)))
    yield

    yg = y * _silu(z)
    ms = jnp.mean(yg * yg, axis=-1, keepdims=True)
    return _bf((yg * lax.rsqrt(ms + EPS)) * norm_w)


def _ssd_mix(zx, params):
    bsz, t, _ = zx.shape
    ct = SSD_CHUNK
    gps = SSD_GROUPS_PER_STEP
    xw = gps * SSD_GROUP_CH
    nw = gps * LANES
    d_inner = params["norm_w"].shape[1]
    nsteps = d_inner // xw
    assert t % ct == 0 and d_inner == SSM_GROUPS * SSD_GROUP_CH and d_inner % xw == 0
    xb = d_inner // xw
    bb = 2 * d_inner // nw
    cbk = bb + nsteps
    dtb = (2 * d_inner + 2 * SSM_GROUPS * SSM_STATE) // LANES

    cw = params["conv_w"]
    cbias = params["conv_b"]
    wide = lambda off: pl.BlockSpec((1, ct, xw), lambda b, g, i: (b, i, off + g))
    narrow = lambda off: pl.BlockSpec((1, ct, nw), lambda b, g, i: (b, i, off + g))
    row_wide = pl.BlockSpec((1, xw), lambda b, g, i: (0, g))
    row_full = pl.BlockSpec((1, LANES), lambda b, g, i: (0, 0))
    cb_b = d_inner // nw
    return pl.pallas_call(
        _ssd_kernel,
        grid=(bsz, nsteps, t // ct),
        in_specs=[
            wide(0), wide(xb), narrow(bb), narrow(cbk),
            pl.BlockSpec((1, ct, LANES), lambda b, g, i: (b, i, dtb)),
            pl.BlockSpec((CONV_K, xw), lambda b, g, i: (0, g)),
            pl.BlockSpec((CONV_K, nw), lambda b, g, i: (0, cb_b + g)),
            pl.BlockSpec((CONV_K, nw), lambda b, g, i: (0, cb_b + nsteps + g)),
            pl.BlockSpec((1, xw), lambda b, g, i: (0, g)),
            pl.BlockSpec((1, nw), lambda b, g, i: (0, cb_b + g)),
            pl.BlockSpec((1, nw), lambda b, g, i: (0, cb_b + nsteps + g)),
            row_full, row_full, row_wide, row_wide,
        ],
        out_specs=pl.BlockSpec((1, ct, xw), lambda b, g, i: (b, i, g)),
        out_shape=jax.ShapeDtypeStruct((bsz, t, d_inner), BF16),
        scratch_shapes=[
            pltpu.VMEM((ct + SUBLANES, xw + 2 * nw), F32),
            pltpu.VMEM((LANES, ct), F32),
            pltpu.VMEM((gps, SSM_STATE, SSD_GROUP_CH), F32),
        ],
        compiler_params=_cparams(("parallel", "parallel", "arbitrary")),
        name="mamba2_ssd_mix",
    )(zx, zx, zx, zx, zx, cw, cw, cw, cbias, cbias, cbias, params["dt_bias_row"], params["a_log_row"],
      params["d_exp"], params["norm_w"])


def _pad_cols(w, n):
    return jnp.pad(w, ((0, 0), (0, n - w.shape[1])))


def _lane_row(v, offset):
    return jnp.pad(v, (offset, LANES - offset - v.shape[0])).reshape(1, LANES)


def _mods(mod, bsz, d):
    mod = mod[:bsz]
    return mod[:, None, 0:d], mod[:, None, d:2 * d], mod[:, None, 2 * d:3 * d]


def kernel(x, c, ada_mix_w, ada_mix_b, ada_ffn_w, ada_ffn_b, hg_w_in, hg_w_out, rwkv_mu, rwkv_w0, rwkv_w2,
           rwkv_a0, rwkv_a2, rwkv_g2, rwkv_k_k, rwkv_k_a, rwkv_r_k, rwkv_ln_w, rwkv_ln_b, gdn_conv_w,
           gdn_a_log, gdn_dt_bias, gdn_norm_w, ssm_w_in, ssm_conv_w, ssm_conv_b, ssm_dt_bias, ssm_a_log,
           ssm_d, ssm_norm_w, ssm_w_out, ffn_w1, ffn_w3, ffn_w2, final_norm_w):
    bsz, t, d = x.shape
    depth = ada_mix_w.shape[0]
    assert bsz <= SUBLANES

    c_pad = jnp.pad(c, ((0, SUBLANES - bsz), (0, 0)))
    mix_mod = _adaln(c_pad, ada_mix_w, ada_mix_b)
    ffn_mod = _adaln(c_pad, ada_ffn_w, ada_ffn_b)
    final_w = final_norm_w.reshape(1, d)
    w1_bf, w3_bf, w2_bf = _bf(ffn_w1), _bf(ffn_w3), _bf(ffn_w2)

    for i in range(depth):
        j = i // 2
        shift, scale, gate = _mods(mix_mod[i], bsz, d)
        if i % 2 == 0:
            rwkv_dim = rwkv_w0.shape[1]
            gdn_dim = GDN_HEADS * GDN_HEAD
            rwkv_cols = 3 * rwkv_dim + rwkv_w2.shape[1] + rwkv_a2.shape[1] + rwkv_g2.shape[1]
            w_in = _bf(hg_w_in[j])
            w_in = jnp.concatenate(
                [w_in[:, rwkv_cols:rwkv_cols + 4 * gdn_dim], w_in[:, :rwkv_cols], w_in[:, rwkv_cols + 4 * gdn_dim:]],
                axis=1)
            p = _norm_proj(x, shift, scale, _pad_cols(w_in, 7680), tm=1024, tn=1536)
            row = lambda v: v.reshape(1, -1)
            y_a = _rwkv_mix(p, 4 * gdn_dim, dict(
                mu=row(rwkv_mu[j]), w0=row(rwkv_w0[j]), a0=row(rwkv_a0[j]), k_k=row(rwkv_k_k[j]),
                k_a=row(rwkv_k_a[j]), r_k=row(rwkv_r_k[j]), ln_w=row(rwkv_ln_w[j]), ln_b=row(rwkv_ln_b[j]),
                w2=_bf(rwkv_w2[j]), a2=_bf(rwkv_a2[j]), g2=_bf(rwkv_g2[j])))
            y_b = _gdn_mix(p, 0, 4 * gdn_dim + rwkv_cols, dict(
                conv_w=gdn_conv_w[j], a_log_row=_lane_row(gdn_a_log[j], GDN_HEADS),
                dt_bias_row=_lane_row(gdn_dt_bias[j], GDN_HEADS), norm_w=row(gdn_norm_w[j])))
            w_out = _bf(hg_w_out[j])
            x = _proj_residual([y_a, y_b], [w_out[:rwkv_dim], w_out[rwkv_dim:rwkv_dim + gdn_dim]], x, gate,
                               tm=1024, tn=1024)
        else:
            d_inner = ssm_norm_w.shape[1]
            zx = _norm_proj(x, shift, scale, _pad_cols(_bf(ssm_w_in[j]), 10752), tm=1024, tn=1536)
            y = _ssd_mix(zx, dict(
                conv_w=ssm_conv_w[j], conv_b=ssm_conv_b[j].reshape(1, -1),
                dt_bias_row=_lane_row(ssm_dt_bias[j], 0), a_log_row=_lane_row(ssm_a_log[j], 0),
                d_exp=jnp.repeat(ssm_d[j], SSM_HEAD).reshape(1, -1),
                norm_w=ssm_norm_w[j].reshape(1, d_inner)))
            x = _proj_residual([y], [_bf(ssm_w_out[j])], x, gate, tm=1024, tn=512)
        shift, scale, gate = _mods(ffn_mod[i], bsz, d)
        x = _ffn(x, shift, scale, gate, w1_bf, w3_bf, w2_bf, i, final_w, final_norm=(i == depth - 1), tm=512, tf=512)
    return x
```

```python
import functools

import jax
import jax.numpy as jnp
from jax import lax
from jax.experimental import pallas as pl
from jax.experimental.pallas import tpu as pltpu

F32 = jnp.float32
BF16 = jnp.bfloat16

EPS = 1e-5
RWKV_HEAD = 64
RWKV_LN_EPS = 64e-5
GDN_HEAD = 128
GDN_HEADS = 8
SSM_HEAD = 64
SSM_GROUPS = 8
SSM_STATE = 128
CONV_K = 4

LANES = 128
SUBLANES = 8
CHUNK = 64
VMEM_LIMIT = 56 * 1024 * 1024


def _cparams(sem):
    return pltpu.CompilerParams(dimension_semantics=sem, vmem_limit_bytes=VMEM_LIMIT)


def _bf(x):
    return x.astype(BF16)


def _dot(a, b):
    return jnp.dot(a, b, preferred_element_type=F32)


def _dot_nt(a, b):
    return lax.dot_general(a, b, (((1,), (1,)), ((), ())), preferred_element_type=F32)


def _dot_tn(a, b):
    return lax.dot_general(a, b, (((0,), (0,)), ((), ())), preferred_element_type=F32)


def _split2(x):
    hi = _bf(x)
    lo = _bf(x - hi.astype(F32))
    return hi, lo


def _split3(x):
    hi = _bf(x)
    r = x - hi.astype(F32)
    mid = _bf(r)
    lo = _bf(r - mid.astype(F32))
    return hi, mid, lo


def _dot3(a, b):
    ah, al = _split2(a)
    bh, bl = _split2(b)
    return _dot(ah, bh) + (_dot(ah, bl) + _dot(al, bh))


def _dot_sel_l(sel, x):
    hi, mid, lo = _split3(x)
    return _dot(sel, hi) + (_dot(sel, mid) + _dot(sel, lo))


def _dot_sel_r(x, sel):
    hi, mid, lo = _split3(x)
    return _dot(hi, sel) + (_dot(mid, sel) + _dot(lo, sel))


def _silu(x):
    h = 0.5 * x
    return h + h * jnp.tanh(h)


def _softplus(x):
    return jnp.maximum(x, 0.0) + jnp.log(1.0 + jnp.exp(-jnp.abs(x)))


def _iota(shape, dim):
    return lax.broadcasted_iota(jnp.int32, shape, dim)


def _chunk_masks(n, chunk):
    ri = _iota((n, n), 0)
    ci = _iota((n, n), 1)
    same = (ri // chunk) == (ci // chunk)
    return same & (ri >= ci), same & (ri > ci)


INV_BASE = 8


def _unit_lower_inverse(a_strict, chunk, mm):
    n = a_strict.shape[0]
    ri = _iota((n, n), 0)
    ci = _iota((n, n), 1)
    q = jnp.where((ri // INV_BASE) == (ci // INV_BASE), -a_strict, 0.0)
    t = (ri == ci).astype(F32) + q
    for _ in range(2):
        q = mm(q, q)
        yield
        t = t + mm(t, q)
        yield
    s = INV_BASE
    while s < chunk:
        off = ((ri // (2 * s)) == (ci // (2 * s))) & ((ri // s) != (ci // s))
        x = mm(jnp.where(off, a_strict, 0.0), t)
        yield
        t = t - mm(t, x)
        yield
        s *= 2
    return t


def _dot1(a, b):
    return _dot(_bf(a), _bf(b))


def _lockstep(gens):
    results = [None] * len(gens)
    live = list(enumerate(gens))
    while live:
        still = []
        for i, gen in live:
            try:
                next(gen)
                still.append((i, gen))
            except StopIteration as done:
                results[i] = done.value
        live = still
        if live:
            yield
    return results


def _interleave(gens):
    runner = _lockstep(gens)
    while True:
        try:
            next(runner)
        except StopIteration as done:
            return done.value


def _rows_bcast(x, rows, chunk):
    n = x.shape[0]
    parts = []
    for c0 in range(0, n, chunk):
        parts.append(jnp.broadcast_to(x[c0 + rows:c0 + rows + 1, :], (chunk, x.shape[1])))
    return jnp.concatenate(parts, axis=0) if len(parts) > 1 else parts[0]


def _adaln_kernel(c_ref, w_ref, b_ref, o_ref):
    o_ref[0] = _dot3(_silu(c_ref[...]), w_ref[0]) + b_ref[0]


def _adaln(c_pad, w, b):
    depth, d, n = w.shape
    tn = 1536
    assert n % tn == 0
    return pl.pallas_call(
        _adaln_kernel,
        grid=(depth, n // tn),
        in_specs=[
            pl.BlockSpec((SUBLANES, d), lambda l, j: (0, 0)),
            pl.BlockSpec((1, d, tn), lambda l, j: (l, 0, j)),
            pl.BlockSpec((1, 1, tn), lambda l, j: (l, 0, j)),
        ],
        out_specs=pl.BlockSpec((1, SUBLANES, tn), lambda l, j: (l, 0, j)),
        out_shape=jax.ShapeDtypeStruct((depth, SUBLANES, n), F32),
        compiler_params=_cparams(("parallel", "parallel")),
        name="adaln_mod",
    )(c_pad, w, b.reshape(depth, 1, n))


def _modulated_norm(x, shift, scale):
    ms = jnp.mean(x * x, axis=-1, keepdims=True)
    return (x * lax.rsqrt(ms + EPS)) * (1.0 + scale) + shift


def _norm_proj_kernel(x_ref, shift_ref, scale_ref, w_ref, o_ref, h_ref):
    @pl.when(pl.program_id(2) == 0)
    def _():
        h_ref[...] = _bf(_modulated_norm(x_ref[0], shift_ref[0], scale_ref[0]))

    o_ref[0] = _dot(h_ref[...], w_ref[...])


def _norm_proj(x, shift, scale, w_bf16, tm, tn):
    bsz, t, d = x.shape
    n = w_bf16.shape[1]
    assert t % tm == 0 and n % tn == 0
    return pl.pallas_call(
        _norm_proj_kernel,
        grid=(bsz, t // tm, n // tn),
        in_specs=[
            pl.BlockSpec((1, tm, d), lambda b, i, j: (b, i, 0)),
            pl.BlockSpec((1, 1, d), lambda b, i, j: (b, 0, 0)),
            pl.BlockSpec((1, 1, d), lambda b, i, j: (b, 0, 0)),
            pl.BlockSpec((d, tn), lambda b, i, j: (0, j)),
        ],
        out_specs=pl.BlockSpec((1, tm, tn), lambda b, i, j: (b, i, j)),
        out_shape=jax.ShapeDtypeStruct((bsz, t, n), F32),
        scratch_shapes=[pltpu.VMEM((tm, d), BF16)],
        compiler_params=_cparams(("parallel", "parallel", "arbitrary")),
        name="norm_proj",
    )(x, shift, scale, w_bf16)


def _proj_residual_kernel(n_in, *refs):
    y_refs = refs[:n_in]
    w_refs = refs[n_in:2 * n_in]
    x_ref, gate_ref, o_ref = refs[2 * n_in:]
    acc = _dot(y_refs[0][0], w_refs[0][...])
    for y_ref, w_ref in zip(y_refs[1:], w_refs[1:]):
        acc = acc + _dot(y_ref[0], w_ref[...])
    o_ref[0] = x_ref[0] + gate_ref[0] * acc


def _proj_residual(ys, ws, x, gate, tm, tn):
    bsz, t, d = x.shape
    n_in = len(ys)
    assert t % tm == 0 and d % tn == 0
    in_specs = [pl.BlockSpec((1, tm, y.shape[2]), lambda b, i, j: (b, i, 0)) for y in ys]
    in_specs += [pl.BlockSpec((w.shape[0], tn), lambda b, i, j: (0, j)) for w in ws]
    in_specs += [
        pl.BlockSpec((1, tm, tn), lambda b, i, j: (b, i, j)),
        pl.BlockSpec((1, 1, tn), lambda b, i, j: (b, 0, j)),
    ]
    return pl.pallas_call(
        functools.partial(_proj_residual_kernel, n_in),
        grid=(bsz, t // tm, d // tn),
        in_specs=in_specs,
        out_specs=pl.BlockSpec((1, tm, tn), lambda b, i, j: (b, i, j)),
        out_shape=jax.ShapeDtypeStruct((bsz, t, d), F32),
        compiler_params=_cparams(("parallel", "parallel", "arbitrary")),
        name="proj_residual",
    )(*ys, *ws, x, gate)


def _ffn_kernel(final_norm, x_ref, shift_ref, scale_ref, gate_ref, w1_ref, w3_ref, w2_ref, fw_ref, o_ref,
                h_ref):
    f = pl.program_id(2)

    @pl.when(f == 0)
    def _():
        h_ref[...] = _bf(_modulated_norm(x_ref[0], shift_ref[0], scale_ref[0]))
        o_ref[...] = jnp.zeros_like(o_ref)

    h = h_ref[...]
    a = _dot(h, w1_ref[...])
    b = _dot(h, w3_ref[...])
    o_ref[0] += _dot(_bf(_silu(a) * b), w2_ref[...])

    @pl.when(f == pl.num_programs(2) - 1)
    def _():
        y = x_ref[0] + gate_ref[0] * o_ref[0]
        if final_norm:
            ms = jnp.mean(y * y, axis=-1, keepdims=True)
            y = (y * lax.rsqrt(ms + EPS)) * fw_ref[...]
        o_ref[0] = y


def _ffn(x, shift, scale, gate, w1, w3, w2, layer, final_w, final_norm, tm, tf):
    bsz, t, d = x.shape
    hidden = w1.shape[2]
    assert t % tm == 0 and hidden % tf == 0
    return pl.pallas_call(
        functools.partial(_ffn_kernel, final_norm),
        grid=(bsz, t // tm, hidden // tf),
        in_specs=[
            pl.BlockSpec((1, tm, d), lambda b, i, f: (b, i, 0)),
            pl.BlockSpec((1, 1, d), lambda b, i, f: (b, 0, 0)),
            pl.BlockSpec((1, 1, d), lambda b, i, f: (b, 0, 0)),
            pl.BlockSpec((1, 1, d), lambda b, i, f: (b, 0, 0)),
            pl.BlockSpec((None, d, tf), lambda b, i, f: (layer, 0, f)),
            pl.BlockSpec((None, d, tf), lambda b, i, f: (layer, 0, f)),
            pl.BlockSpec((None, tf, d), lambda b, i, f: (layer, f, 0)),
            pl.BlockSpec((1, d), lambda b, i, f: (0, 0)),
        ],
        out_specs=pl.BlockSpec((1, tm, d), lambda b, i, f: (b, i, 0)),
        out_shape=jax.ShapeDtypeStruct((bsz, t, d), F32),
        scratch_shapes=[pltpu.VMEM((tm, d), BF16)],
        compiler_params=_cparams(("parallel", "parallel", "arbitrary")),
        name="swiglu_ffn",
    )(x, shift, scale, gate, w1, w3, w2, final_w)


RWKV_ROWS = 4 * CHUNK
RWKV_PAIRS = 8


def _head_half_sum(x, first_half):
    s0 = jnp.sum(jnp.where(first_half, x, 0.0), axis=-1, keepdims=True)
    s1 = jnp.sum(jnp.where(first_half, 0.0, x), axis=-1, keepdims=True)
    return jnp.where(first_half, s0, s1)


def _rwkv_kernel(pr_ref, pk_ref, pv_ref, pl_ref, mur_ref, muk_ref, muv_ref, mul_ref, w0_ref, a0_ref, kkw_ref,
                 kaw_ref, rkw_ref, lnw_ref, lnb_ref, w2_ref, a2_ref, g2_ref, o_ref, sh_ref, s_ref):
    ct = RWKV_ROWS
    wb = RWKV_PAIRS * LANES
    t_idx = pl.program_id(2)

    @pl.when(t_idx == 0)
    def _():
        sh_ref[pl.ds(0, SUBLANES), :] = jnp.zeros((SUBLANES, sh_ref.shape[1]), F32)
        s_ref[...] = jnp.zeros_like(s_ref)

    sh_ref[pl.ds(SUBLANES, ct), 0:wb] = pr_ref[0]
    sh_ref[pl.ds(SUBLANES, ct), wb:2 * wb] = pk_ref[0]
    sh_ref[pl.ds(SUBLANES, ct), 2 * wb:3 * wb] = pv_ref[0]
    sh_ref[pl.ds(SUBLANES, ct), 3 * wb:3 * wb + 256] = pl_ref[0]

    def lerp(lo, hi, mu_ref):
        cur = sh_ref[pl.ds(SUBLANES, ct), lo:hi]
        prev = sh_ref[pl.ds(SUBLANES - 1, ct), lo:hi]
        return cur + mu_ref[...] * (prev - cur)

    r = lerp(0, wb, mur_ref)
    k = lerp(wb, 2 * wb, muk_ref)
    v = lerp(2 * wb, 3 * wb, muv_ref)
    xl = lerp(3 * wb, 3 * wb + 256, mul_ref)
    sh_ref[pl.ds(0, SUBLANES), :] = sh_ref[pl.ds(ct, SUBLANES), :]
    pw = xl[:, 0:64]
    pa = xl[:, 64:128]
    pg = xl[:, 128:256]

    w = -_softplus(-(w0_ref[...] + _dot(_bf(jnp.tanh(pw)), w2_ref[...]))) - 0.5
    logd = -jnp.exp(w)
    a = jax.nn.sigmoid(a0_ref[...] + _dot(_bf(pa), a2_ref[...]))
    g = _dot(_bf(jax.nn.sigmoid(pg)), g2_ref[...])

    gens = []
    for p in range(RWKV_PAIRS):
        cols = slice(p * LANES, (p + 1) * LANES)
        gens.append(_rwkv_pair(
            r[:, cols], k[:, cols], v[:, cols], logd[:, cols], a[:, cols], g[:, cols], kkw_ref[:, cols],
            kaw_ref[:, cols], rkw_ref[:, cols], lnw_ref[:, cols], lnb_ref[:, cols], s_ref.at[p]))
    for p, out in enumerate(_interleave(gens)):
        o_ref[0, :, p * LANES:(p + 1) * LANES] = out


def _rwkv_chunk_prep(kkt, rt, bt, kt, kk0, r0, v):
    fh = _iota((CHUNK, LANES), 1) < RWKV_HEAD
    ns = 2 * CHUNK

    def stack(x, masked):
        if masked:
            return jnp.concatenate([jnp.where(fh, x, 0.0), jnp.where(fh, 0.0, x)], axis=0)
        return jnp.concatenate([x, x], axis=0)

    def unstack(x_st):
        return x_st[:CHUNK] + x_st[CHUNK:]

    lh, ll = _split2(jnp.concatenate([stack(kkt, True), stack(rt, True)], axis=0))
    rh, rl = _split2(jnp.concatenate([stack(bt, False), stack(kt, False)], axis=0))
    abig = _dot_nt(lh, rh) + (_dot_nt(lh, rl) + _dot_nt(ll, rh))
    yield
    incl, strict = _chunk_masks(ns, CHUNK)
    a_ab = jnp.where(strict, abig[:ns, :ns], 0.0)
    a_ak = jnp.where(strict, abig[:ns, ns:], 0.0)
    a_rb = jnp.where(incl, abig[ns:, :ns], 0.0)
    a_rk = jnp.where(incl, abig[ns:, ns:], 0.0)

    v_st = stack(v, True)
    av = _dot3(a_ak, v_st)
    ark = _dot(_bf(a_rk), _bf(v_st))
    tmat = yield from _unit_lower_inverse(a_ab, CHUNK, _dot3)
    sol = _dot3(tmat, jnp.concatenate([stack(kk0, True), av], axis=1))
    yield
    ar = _dot(_bf(a_rb), _bf(sol))
    yield
    return (unstack(sol[:, :LANES]), unstack(sol[:, LANES:]), r0 - unstack(ar[:, :LANES]),
            unstack(ark - ar[:, LANES:]))


def _rwkv_pair(r, k, v, logd, a, g, kkw, kaw, rkw, lnw, lnb, s_ref):
    ct = RWKV_ROWS
    n_chunks = ct // CHUNK
    first_half = _iota((ct, LANES), 1) < RWKV_HEAD
    kk = k * kkw
    kk = kk * lax.rsqrt(_head_half_sum(kk * kk, first_half) + 1e-24)
    k2 = k * (1.0 + (a - 1.0) * kaw)
    b = kk * a

    incl_ct, _ = _chunk_masks(ct, CHUNK)
    c = _dot_sel_l(_bf(incl_ct.astype(F32)), logd)
    yield
    cm1 = c - logd
    cref = _rows_bcast(c, CHUNK // 2 - 1, CHUNK)
    cend = _rows_bcast(c, CHUNK - 1, CHUNK)
    e_neg = jnp.exp(cref - c)
    rt = r * jnp.exp(c - cref)
    kkt = kk * jnp.exp(cm1 - cref)
    bt = b * e_neg
    kt = k2 * e_neg
    kk0 = kk * jnp.exp(cm1)
    r0 = r * jnp.exp(c)
    e_end = jnp.exp(cend - c)
    bh = b * e_end
    kh = k2 * e_end
    dc = jnp.exp(cend)

    preps = yield from _lockstep([
        _rwkv_chunk_prep(*(x[ci * CHUNK:(ci + 1) * CHUNK] for x in (kkt, rt, bt, kt, kk0, r0, v)))
        for ci in range(n_chunks)])

    ri = _iota((LANES, LANES), 0)
    cj = _iota((LANES, LANES), 1)
    same_head = (ri // RWKV_HEAD) == (cj // RWKV_HEAD)

    ys = []
    for ci in range(n_chunks):
        rows = slice(ci * CHUNK, (ci + 1) * CHUNK)
        wm_c, uv_c, rm_c, yv_c = preps[ci]
        s_old = s_ref[...]
        s_bf = _bf(s_old)
        ys.append(_dot_nt(_bf(rm_c), s_bf) + yv_c)
        m_t = jnp.where(same_head, _dot_tn(_bf(wm_c), _bf(bh[rows])), 0.0)
        hv_t = jnp.where(
            same_head,
            _dot_tn(_bf(jnp.concatenate([v[rows], -uv_c], axis=0)),
                    _bf(jnp.concatenate([kh[rows], bh[rows]], axis=0))),
            0.0)
        yield
        s_ref[...] = s_old * dc[ci * CHUNK:ci * CHUNK + 1, :] - _dot(s_bf, _bf(m_t)) + hv_t
        yield
    y = jnp.concatenate(ys, axis=0)

    inv_n = 1.0 / RWKV_HEAD
    mean = _head_half_sum(y, first_half) * inv_n
    yc = y - mean
    var = _head_half_sum(yc * yc, first_half) * inv_n
    yn = yc * lax.rsqrt(var + RWKV_LN_EPS) * lnw + lnb
    bonus = _head_half_sum(r * k2 * rkw, first_half) * v
    return _bf((yn + bonus) * g)


def _rwkv_mix(p, col0, params):
    bsz, t, _ = p.shape
    dim = params["w0"].shape[1]
    wb = RWKV_PAIRS * LANES
    nb = dim // wb
    ct = RWKV_ROWS
    assert t % ct == 0 and col0 % wb == 0 and (col0 + 3 * dim) % 256 == 0 and dim % wb == 0
    cb0 = col0 // wb
    lb0 = (col0 + 3 * dim) // 256

    def col_spec(off):
        return pl.BlockSpec((1, ct, wb), lambda b, h, i: (b, i, cb0 + off + h))

    def row_spec(off):
        return pl.BlockSpec((1, wb), lambda b, h, i: (0, off + h))

    mu = params["mu"]
    in_specs = [
        col_spec(0), col_spec(nb), col_spec(2 * nb),
        pl.BlockSpec((1, ct, 256), lambda b, h, i: (b, i, lb0)),
        row_spec(0), row_spec(nb), row_spec(2 * nb),
        pl.BlockSpec((1, 256), lambda b, h, i: (0, 3 * dim // 256)),
        row_spec(0), row_spec(0), row_spec(0), row_spec(0), row_spec(0), row_spec(0), row_spec(0),
        pl.BlockSpec((params["w2"].shape[0], wb), lambda b, h, i: (0, h)),
        pl.BlockSpec((params["a2"].shape[0], wb), lambda b, h, i: (0, h)),
        pl.BlockSpec((params["g2"].shape[0], wb), lambda b, h, i: (0, h)),
    ]
    return pl.pallas_call(
        _rwkv_kernel,
        grid=(bsz, nb, t // ct),
        in_specs=in_specs,
        out_specs=pl.BlockSpec((1, ct, wb), lambda b, h, i: (b, i, h)),
        out_shape=jax.ShapeDtypeStruct((bsz, t, dim), BF16),
        scratch_shapes=[pltpu.VMEM((ct + SUBLANES, 3 * wb + 256), F32),
                        pltpu.VMEM((RWKV_PAIRS, LANES, LANES), F32)],
        compiler_params=_cparams(("parallel", "parallel", "arbitrary")),
        name="rwkv7_mix",
    )(p, p, p, p, mu, mu, mu, mu, params["w0"], params["a0"], params["k_k"], params["k_a"], params["r_k"],
      params["ln_w"], params["ln_b"], params["w2"], params["a2"], params["g2"])


GDN_ROWS = 4 * CHUNK
GDN_HEADS_PER_STEP = 8


def _causal_conv(sh_ref, lo, hi, w, n_rows):
    assert CONV_K == 4
    ext = sh_ref[pl.ds(0, n_rows + SUBLANES), lo:hi]
    prev = pltpu.roll(ext, 1, axis=0)
    older = pltpu.roll(prev * w[0:1, :] + ext * w[1:2, :], 2, axis=0)
    return (older + (prev * w[2:3, :] + ext * w[3:4, :]))[SUBLANES:]


def _gdn_kernel(q_ref, k_ref, v_ref, z_ref, gt_ref, cwq_ref, cwk_ref, cwv_ref, alog_ref, dtb_ref, nw_ref,
                o_ref, sh_ref, s_ref):
    ct = GDN_ROWS
    wb = GDN_HEADS_PER_STEP * LANES
    t_idx = pl.program_id(2)

    @pl.when(t_idx == 0)
    def _():
        sh_ref[pl.ds(0, SUBLANES), :] = jnp.zeros((SUBLANES, sh_ref.shape[1]), F32)
        s_ref[...] = jnp.zeros_like(s_ref)

    sh_ref[pl.ds(SUBLANES, ct), 0:wb] = q_ref[0]
    sh_ref[pl.ds(SUBLANES, ct), wb:2 * wb] = k_ref[0]
    sh_ref[pl.ds(SUBLANES, ct), 2 * wb:3 * wb] = v_ref[0]
    q = _silu(_causal_conv(sh_ref, 0, wb, cwq_ref[...], ct))
    k = _silu(_causal_conv(sh_ref, wb, 2 * wb, cwk_ref[...], ct))
    v = _silu(_causal_conv(sh_ref, 2 * wb, 3 * wb, cwv_ref[...], ct))
    sh_ref[pl.ds(0, SUBLANES), :] = sh_ref[pl.ds(ct, SUBLANES), :]

    gates = gt_ref[0]
    beta_all = jax.nn.sigmoid(gates)
    g_all = -jnp.exp(alog_ref[...]) * _softplus(gates + dtb_ref[...])
    z = z_ref[0]
    gens = []
    for p in range(GDN_HEADS_PER_STEP):
        cols = slice(p * LANES, (p + 1) * LANES)
        h_idx = pl.program_id(1) * GDN_HEADS_PER_STEP + p
        gens.append(_gdn_head(q[:, cols], k[:, cols], v[:, cols], z[:, cols], beta_all, g_all, h_idx,
                              nw_ref[...], s_ref.at[p]))
    for p, out in enumerate(_interleave(gens)):
        o_ref[0, :, p * LANES:(p + 1) * LANES] = out


def _gdn_tile_prep(q, k, kb, vb, kbe, gc):
    n = 2 * CHUNK
    incl, strict = _chunk_masks(n, CHUNK)
    decay = jnp.where(incl, jnp.exp(jnp.minimum(gc - gc.T, 0.0)), 0.0)
    kq = _dot_nt(_bf(jnp.concatenate([kb, q], axis=0)), _bf(k))
    yield
    a_mat = jnp.where(strict, kq[:n] * decay, 0.0)
    qk = kq[n:] * decay
    tmat = yield from _unit_lower_inverse(a_mat, CHUNK, _dot1)
    sol = _dot(_bf(tmat), _bf(jnp.concatenate([vb, kbe], axis=1)))
    yield
    qu = _dot(_bf(qk), _bf(sol))
    yield
    return sol, qu


def _gdn_head(q, k, v, z, beta_all, g_all, h_idx, norm_w, s_ref):
    ct = GDN_ROWS
    n_chunks = ct // CHUNK
    q = q * lax.rsqrt(jnp.sum(q * q, axis=-1, keepdims=True) + 1e-6) * (GDN_HEAD ** -0.5)
    k = k * lax.rsqrt(jnp.sum(k * k, axis=-1, keepdims=True) + 1e-6)
    lane = _iota((ct, LANES), 1)
    beta = jnp.sum(jnp.where(lane == h_idx, beta_all, 0.0), axis=-1, keepdims=True)
    g = jnp.sum(jnp.where(lane == h_idx + GDN_HEADS, g_all, 0.0), axis=-1, keepdims=True)

    incl, _ = _chunk_masks(ct, CHUNK)
    gc = _dot_sel_l(_bf(incl.astype(F32)), jnp.broadcast_to(g, (ct, LANES)))
    yield
    kb = k * beta
    egc = jnp.exp(gc)
    vb = v * beta
    kbe = kb * egc
    tiles = yield from _lockstep([
        _gdn_tile_prep(*(x[t0:t0 + LANES] for x in (q, k, kb, vb, kbe, gc))) for t0 in range(0, ct, LANES)])
    sol = jnp.concatenate([s for s, _ in tiles], axis=0)
    qu = jnp.concatenate([u for _, u in tiles], axis=0)
    o_v = qu[:, :LANES]
    o_m = q * egc - qu[:, LANES:]
    gl = _rows_bcast(gc, CHUNK - 1, CHUNK)
    kd = k * jnp.exp(gl - gc)
    egl = jnp.exp(gl)

    outs = []
    for ci in range(n_chunks):
        rows = slice(ci * CHUNK, (ci + 1) * CHUNK)
        s_old = s_ref[...]
        s_bf = _bf(s_old)
        outs.append(_dot(_bf(o_m[rows]), s_bf) + o_v[rows])
        hm = _dot_tn(_bf(kd[rows]), _bf(sol[rows]))
        yield
        s_ref[...] = s_old * egl[ci * CHUNK:ci * CHUNK + 1, :] - _dot(_bf(hm[:, LANES:]), s_bf) + hm[:, :LANES]
        yield
    o = jnp.concatenate(outs, axis=0)

    ms = jnp.mean(o * o, axis=-1, keepdims=True)
    return _bf((o * lax.rsqrt(ms + EPS)) * norm_w * _silu(z))


def _gdn_mix(p, col0, gate_col, params):
    bsz, t, _ = p.shape
    ct = GDN_ROWS
    wb = GDN_HEADS_PER_STEP * LANES
    dim = GDN_HEADS * GDN_HEAD
    nb = dim // wb
    assert t % ct == 0 and col0 % wb == 0 and dim % wb == 0 and gate_col % LANES == 0
    cb0 = col0 // wb
    gate_blk = gate_col // LANES

    def col_spec(off):
        return pl.BlockSpec((1, ct, wb), lambda b, h, i: (b, i, cb0 + off + h))

    def cw_spec(off):
        return pl.BlockSpec((CONV_K, wb), lambda b, h, i: (0, off + h))

    full_row = pl.BlockSpec((1, LANES), lambda b, h, i: (0, 0))
    cw = params["conv_w"]
    return pl.pallas_call(
        _gdn_kernel,
        grid=(bsz, nb, t // ct),
        in_specs=[
            col_spec(0), col_spec(nb), col_spec(2 * nb), col_spec(3 * nb),
            pl.BlockSpec((1, ct, LANES), lambda b, h, i: (b, i, gate_blk)),
            cw_spec(0), cw_spec(nb), cw_spec(2 * nb),
            full_row, full_row, full_row,
        ],
        out_specs=pl.BlockSpec((1, ct, wb), lambda b, h, i: (b, i, h)),
        out_shape=jax.ShapeDtypeStruct((bsz, t, dim), BF16),
        scratch_shapes=[pltpu.VMEM((ct + SUBLANES, 3 * wb), F32),
                        pltpu.VMEM((GDN_HEADS_PER_STEP, GDN_HEAD, GDN_HEAD), F32)],
        compiler_params=_cparams(("parallel", "parallel", "arbitrary")),
        name="gated_deltanet_mix",
    )(p, p, p, p, p, cw, cw, cw, params["a_log_row"], params["dt_bias_row"], params["norm_w"])


SSD_CHUNK = 128
SSD_GROUP_CH = 512
SSD_GROUPS_PER_STEP = 8
LOG2E = 1.4426950408889634


def _ssd_kernel(z_ref, x_ref, b_ref, c_ref, dt_ref, cwx_ref, cwb_ref, cwc_ref, cbx_ref, cbb_ref, cbc_ref,
                dtb_ref, alogs_ref, dsk_ref, nw_ref, o_ref, sh_ref, at_ref, st_ref):
    ct = SSD_CHUNK
    gch = SSD_GROUP_CH
    ng = SSD_GROUPS_PER_STEP
    hpg = gch // SSM_HEAD
    xw = ng * gch
    t_idx = pl.program_id(2)

    @pl.when(t_idx == 0)
    def _():
        sh_ref[pl.ds(0, SUBLANES), :] = jnp.zeros((SUBLANES, sh_ref.shape[1]), F32)
        st_ref[...] = jnp.zeros_like(st_ref)

    sh_ref[pl.ds(SUBLANES, ct), 0:xw] = x_ref[0]
    sh_ref[pl.ds(SUBLANES, ct), xw:xw + ng * LANES] = b_ref[0]
    sh_ref[pl.ds(SUBLANES, ct), xw + ng * LANES:xw + 2 * ng * LANES] = c_ref[0]
    xs = _silu(_causal_conv(sh_ref, 0, xw, cwx_ref[...], ct) + cbx_ref[...])
    bm = _silu(_causal_conv(sh_ref, xw, xw + ng * LANES, cwb_ref[...], ct) + cbb_ref[...])
    cm = _silu(_causal_conv(sh_ref, xw + ng * LANES, xw + 2 * ng * LANES, cwc_ref[...], ct) + cbc_ref[...])
    sh_ref[pl.ds(0, SUBLANES), :] = sh_ref[pl.ds(ct, SUBLANES), :]

    dtv = _softplus(dt_ref[0] + dtb_ref[...])
    ri = _iota((ct, ct), 0)
    ci = _iota((ct, ct), 1)
    a_small = dtv * (-jnp.exp(alogs_ref[...]) * LOG2E)
    acum_small = _dot_sel_l(_bf((ri >= ci).astype(F32)), a_small)
    at_ref[...] = acum_small.T
    head_cols = _split3(jnp.concatenate([dtv, acum_small], axis=0))
    z = z_ref[0]
    gens = []
    for i in range(ng):
        g_idx = pl.program_id(1) * ng + i
        wide = slice(i * gch, (i + 1) * gch)
        nar = slice(i * LANES, (i + 1) * LANES)
        acum_rows = at_ref[pl.ds(pl.multiple_of(g_idx * hpg, hpg), hpg), :]
        gens.append(_ssd_group(g_idx, xs[:, wide], bm[:, nar], cm[:, nar], z[:, wide], head_cols, acum_rows,
                               dsk_ref[:, wide], nw_ref[:, wide], st_ref.at[i]))
    for i, out in enumerate(_interleave(gens)):
        o_ref[0, :, i * gch:(i + 1) * gch] = out


def _ssd_group(g_idx, xs, bm, cm, z, head_cols, acum_rows, dskip, norm_w, st_ref):
    ct = SSD_CHUNK
    gch = SSD_GROUP_CH
    hpg = gch // SSM_HEAD
    expand = _bf((_iota((LANES, gch), 0) == g_idx * hpg + _iota((LANES, gch), 1) // SSM_HEAD).astype(F32))
    hi, mid, lo = head_cols
    both = _dot(hi, expand) + (_dot(mid, expand) + _dot(lo, expand))
    yield
    dt_exp = both[:ct]
    acum = both[ct:]
    xdt = xs * dt_exp

    incl = _iota((ct, ct), 0) >= _iota((ct, ct), 1)
    cb = jnp.where(incl, _dot_nt(_bf(cm), _bf(bm)), 0.0)
    st_old = st_ref[...]
    y_state = _dot(_bf(cm), _bf(st_old)) * jnp.exp2(acum)
    yield

    first_half = _iota((ct, LANES), 1) < SSM_HEAD
    y_parts = []
    for jp in range(hpg // 2):
        x_pair = _bf(xdt[:, jp * LANES:(jp + 1) * LANES])
        pair = []
        for j in (2 * jp, 2 * jp + 1):
            col = jnp.broadcast_to(acum[:, j * SSM_HEAD:j * SSM_HEAD + 1], (ct, ct))
            row = jnp.broadcast_to(acum_rows[j:j + 1, :], (ct, ct))
            pair.append(_dot(_bf(cb * jnp.exp2(jnp.minimum(col - row, 0.0))), x_pair))
        y_parts.append(jnp.where(first_half, pair[0], pair[1]))
        yield
    y = jnp.concatenate(y_parts, axis=1) + y_state + xs * dskip

    a_last = acum[ct - 1:ct, :]
    st_ref[...] = st_old * jnp.exp2(a_last) + _dot_tn(_bf(bm), _bf(xdt * jnp.exp2(a_last - acum)))
    yield

    yg = y * _silu(z)
    ms = jnp.mean(yg * yg, axis=-1, keepdims=True)
    return _bf((yg * lax.rsqrt(ms + EPS)) * norm_w)


def _ssd_mix(zx, params):
    bsz, t, _ = zx.shape
    ct = SSD_CHUNK
    gps = SSD_GROUPS_PER_STEP
    xw = gps * SSD_GROUP_CH
    nw = gps * LANES
    d_inner = params["norm_w"].shape[1]
    nsteps = d_inner // xw
    assert t % ct == 0 and d_inner == SSM_GROUPS * SSD_GROUP_CH and d_inner % xw == 0
    xb = d_inner // xw
    bb = 2 * d_inner // nw
    cbk = bb + nsteps
    dtb = (2 * d_inner + 2 * SSM_GROUPS * SSM_STATE) // LANES

    cw = params["conv_w"]
    cbias = params["conv_b"]
    wide = lambda off: pl.BlockSpec((1, ct, xw), lambda b, g, i: (b, i, off + g))
    narrow = lambda off: pl.BlockSpec((1, ct, nw), lambda b, g, i: (b, i, off + g))
    row_wide = pl.BlockSpec((1, xw), lambda b, g, i: (0, g))
    row_full = pl.BlockSpec((1, LANES), lambda b, g, i: (0, 0))
    cb_b = d_inner // nw
    return pl.pallas_call(
        _ssd_kernel,
        grid=(bsz, nsteps, t // ct),
        in_specs=[
            wide(0), wide(xb), narrow(bb), narrow(cbk),
            pl.BlockSpec((1, ct, LANES), lambda b, g, i: (b, i, dtb)),
            pl.BlockSpec((CONV_K, xw), lambda b, g, i: (0, g)),
            pl.BlockSpec((CONV_K, nw), lambda b, g, i: (0, cb_b + g)),
            pl.BlockSpec((CONV_K, nw), lambda b, g, i: (0, cb_b + nsteps + g)),
            pl.BlockSpec((1, xw), lambda b, g, i: (0, g)),
            pl.BlockSpec((1, nw), lambda b, g, i: (0, cb_b + g)),
            pl.BlockSpec((1, nw), lambda b, g, i: (0, cb_b + nsteps + g)),
            row_full, row_full, row_wide, row_wide,
        ],
        out_specs=pl.BlockSpec((1, ct, xw), lambda b, g, i: (b, i, g)),
        out_shape=jax.ShapeDtypeStruct((bsz, t, d_inner), BF16),
        scratch_shapes=[
            pltpu.VMEM((ct + SUBLANES, xw + 2 * nw), F32),
            pltpu.VMEM((LANES, ct), F32),
            pltpu.VMEM((gps, SSM_STATE, SSD_GROUP_CH), F32),
        ],
        compiler_params=_cparams(("parallel", "parallel", "arbitrary")),
        name="mamba2_ssd_mix",
    )(zx, zx, zx, zx, zx, cw, cw, cw, cbias, cbias, cbias, params["dt_bias_row"], params["a_log_row"],
      params["d_exp"], params["norm_w"])


def _pad_cols(w, n):
    return jnp.pad(w, ((0, 0), (0, n - w.shape[1])))


def _lane_row(v, offset):
    return jnp.pad(v, (offset, LANES - offset - v.shape[0])).reshape(1, LANES)


def _mods(mod, bsz, d):
    mod = mod[:bsz]
    return mod[:, None, 0:d], mod[:, None, d:2 * d], mod[:, None, 2 * d:3 * d]


def kernel(x, c, ada_mix_w, ada_mix_b, ada_ffn_w, ada_ffn_b, hg_w_in, hg_w_out, rwkv_mu, rwkv_w0, rwkv_w2,
           rwkv_a0, rwkv_a2, rwkv_g2, rwkv_k_k, rwkv_k_a, rwkv_r_k, rwkv_ln_w, rwkv_ln_b, gdn_conv_w,
           gdn_a_log, gdn_dt_bias, gdn_norm_w, ssm_w_in, ssm_conv_w, ssm_conv_b, ssm_dt_bias, ssm_a_log,
           ssm_d, ssm_norm_w, ssm_w_out, ffn_w1, ffn_w3, ffn_w2, final_norm_w):
    bsz, t, d = x.shape
    depth = ada_mix_w.shape[0]
    assert bsz <= SUBLANES

    c_pad = jnp.pad(c, ((0, SUBLANES - bsz), (0, 0)))
    mix_mod = _adaln(c_pad, ada_mix_w, ada_mix_b)
    ffn_mod = _adaln(c_pad, ada_ffn_w, ada_ffn_b)
    final_w = final_norm_w.reshape(1, d)
    w1_bf, w3_bf, w2_bf = _bf(ffn_w1), _bf(ffn_w3), _bf(ffn_w2)

    for i in range(depth):
        j = i // 2
        shift, scale, gate = _mods(mix_mod[i], bsz, d)
        if i % 2 == 0:
            rwkv_dim = rwkv_w0.shape[1]
            gdn_dim = GDN_HEADS * GDN_HEAD
            rwkv_cols = 3 * rwkv_dim + rwkv_w2.shape[1] + rwkv_a2.shape[1] + rwkv_g2.shape[1]
            w_in = _bf(hg_w_in[j])
            w_in = jnp.concatenate(
                [w_in[:, rwkv_cols:rwkv_cols + 4 * gdn_dim], w_in[:, :rwkv_cols], w_in[:, rwkv_cols + 4 * gdn_dim:]],
                axis=1)
            p = _norm_proj(x, shift, scale, _pad_cols(w_in, 7680), tm=1024, tn=1536)
            row = lambda v: v.reshape(1, -1)
            y_a = _rwkv_mix(p, 4 * gdn_dim, dict(
                mu=row(rwkv_mu[j]), w0=row(rwkv_w0[j]), a0=row(rwkv_a0[j]), k_k=row(rwkv_k_k[j]),
                k_a=row(rwkv_k_a[j]), r_k=row(rwkv_r_k[j]), ln_w=row(rwkv_ln_w[j]), ln_b=row(rwkv_ln_b[j]),
                w2=_bf(rwkv_w2[j]), a2=_bf(rwkv_a2[j]), g2=_bf(rwkv_g2[j])))
            y_b = _gdn_mix(p, 0, 4 * gdn_dim + rwkv_cols, dict(
                conv_w=gdn_conv_w[j], a_log_row=_lane_row(gdn_a_log[j], GDN_HEADS),
                dt_bias_row=_lane_row(gdn_dt_bias[j], GDN_HEADS), norm_w=row(gdn_norm_w[j])))
            w_out = _bf(hg_w_out[j])
            x = _proj_residual([y_a, y_b], [w_out[:rwkv_dim], w_out[rwkv_dim:rwkv_dim + gdn_dim]], x, gate,
                               tm=1024, tn=1024)
        else:
            d_inner = ssm_norm_w.shape[1]
            zx = _norm_proj(x, shift, scale, _pad_cols(_bf(ssm_w_in[j]), 10752), tm=1024, tn=1536)
            y = _ssd_mix(zx, dict(
                conv_w=ssm_conv_w[j], conv_b=ssm_conv_b[j].reshape(1, -1),
                dt_bias_row=_lane_row(ssm_dt_bias[j], 0), a_log_row=_lane_row(ssm_a_log[j], 0),
                d_exp=jnp.repeat(ssm_d[j], SSM_HEAD).reshape(1, -1),
                norm_w=ssm_norm_w[j].reshape(1, d_inner)))
            x = _proj_residual([y], [_bf(ssm_w_out[j])], x, gate, tm=1024, tn=512)
        shift, scale, gate = _mods(ffn_mod[i], bsz, d)
        x = _ffn(x, shift, scale, gate, w1_bf, w3_bf, w2_bf, i, final_w, final_norm=(i == depth - 1), tm=512, tf=512)
    return x
```
